```python
import math
import jax, jax.numpy as jnp
from jax import lax
import numpy as np

D_MODEL = 1024
BATCH = 8
SEQ = 4096
DEPTH = 4

EPS = 1e-6
D_MIX = D_MODEL

SSD_HEADS = 8
SSD_HEAD_DIM = 64
SSD_INNER = SSD_HEADS * SSD_HEAD_DIM
SSD_GROUPS = 2
SSD_HPG = SSD_HEADS // SSD_GROUPS
SSD_STATE = 128
SSD_CONV = 4
SSD_CHUNK = 128
SSD_XBC = SSD_INNER + 2 * SSD_GROUPS * SSD_STATE
DT_MIN = 0.001
DT_MAX = 0.1

MLA_HEADS = 4
MLA_NOPE = 64
MLA_ROPE = 32
MLA_V = 64
MLA_Q_RANK = 256
MLA_KV_RANK = 128
ROPE_THETA = 10000.0
MAX_POS_OFFSET = 1024

SWA_HEADS = 4
SWA_KV_HEADS = 2
SWA_HEAD_DIM = 64
WINDOW = 128

ATTN_BLOCK = 128

SSD_IN = SSD_INNER + SSD_XBC + SSD_HEADS
MLA_IN = MLA_Q_RANK + MLA_KV_RANK + MLA_ROPE
SWA_IN = (SWA_HEADS + 2 * SWA_KV_HEADS) * SWA_HEAD_DIM
D_IN = SSD_IN + MLA_IN + SWA_IN

D_FF = 2816
FFN_CONV = 3

kernel_name = "hybrid_ssd_mla_swa_convffn_adaln"


def rmsnorm(x, g):
    xf = x.astype(jnp.float32)
    y = xf * lax.rsqrt(jnp.mean(xf * xf, axis=-1, keepdims=True) + EPS)
    return (y * g.astype(jnp.float32)).astype(x.dtype)


def causal_dwconv(x, w, b):
    k_width = w.shape[0]
    s = x.shape[1]
    xp = jnp.pad(x, ((0, 0), (k_width - 1, 0), (0, 0)))
    y = b
    for k in range(k_width):
        y = y + xp[:, k:k + s] * w[k]
    return y


def apply_rope(x, cos, sin):
    xf = x.astype(jnp.float32)
    x1, x2 = jnp.split(xf, 2, axis=-1)
    return jnp.concatenate([x1 * cos - x2 * sin, x1 * sin + x2 * cos], axis=-1).astype(x.dtype)


def ssd_mixer(p, conv_w, conv_b, dt_bias, a_log, d_skip, norm_g):
    f32 = jnp.float32
    bsz, s, _ = p.shape
    nc, q = s // SSD_CHUNK, SSD_CHUNK
    z, xbc, dt = jnp.split(p, [SSD_INNER, SSD_INNER + SSD_XBC], axis=-1)
    xbc = jax.nn.silu(causal_dwconv(xbc, conv_w, conv_b))
    xs, bm, cm = jnp.split(xbc, [SSD_INNER, SSD_INNER + SSD_GROUPS * SSD_STATE], axis=-1)
    xs = xs.astype(f32).reshape(bsz, nc, q, SSD_GROUPS, SSD_HPG, SSD_HEAD_DIM)
    bm = bm.astype(f32).reshape(bsz, nc, q, SSD_GROUPS, SSD_STATE)
    cm = cm.astype(f32).reshape(bsz, nc, q, SSD_GROUPS, SSD_STATE)
    dt = jax.nn.softplus(dt.astype(f32) + dt_bias.astype(f32))
    a = -jnp.exp(a_log.astype(f32))
    dt_c = dt.reshape(bsz, nc, q, SSD_GROUPS, SSD_HPG)
    xdt = xs * dt_c[..., None]
    da = (dt * a).reshape(bsz, nc, q, SSD_HEADS).transpose(0, 1, 3, 2)
    a_h = jnp.cumsum(da, axis=-1).reshape(bsz, nc, SSD_GROUPS, SSD_HPG, q)
    causal = jnp.tril(jnp.ones((q, q), dtype=bool))
    seg = a_h[..., :, None] - a_h[..., None, :]
    decay = jnp.exp(jnp.where(causal, seg, -jnp.inf))
    cb = jnp.einsum('bclgn,bcsgn->bcgls', cm, bm)
    y_diag = jnp.einsum('bcghls,bcsghp->bclghp', cb[:, :, :, None] * decay, xdt)
    decay_to_end = jnp.exp(a_h[..., -1:] - a_h)
    states = jnp.einsum('bcsgn,bcghs,bcsghp->bcghpn', bm, decay_to_end, xdt)
    chunk_decay = jnp.exp(a_h[..., -1])

    def step(h, inp):
        st, dec = inp
        return h * dec[..., None, None] + st, h

    h0 = jnp.zeros((bsz, SSD_GROUPS, SSD_HPG, SSD_HEAD_DIM, SSD_STATE), f32)
    _, h_in = lax.scan(step, h0, (states.swapaxes(0, 1), chunk_decay.swapaxes(0, 1)))
    h_in = h_in.swapaxes(0, 1)
    y_off = jnp.einsum('bclgn,bcghpn,bcghl->bclghp', cm, h_in, jnp.exp(a_h))
    y = y_diag + y_off + xs * d_skip.astype(f32).reshape(SSD_GROUPS, SSD_HPG, 1)
    y = y.reshape(bsz, s, SSD_INNER) * jax.nn.silu(z.astype(f32))
    yg = y.reshape(bsz, s, SSD_GROUPS, SSD_INNER // SSD_GROUPS)
    yg = yg * lax.rsqrt(jnp.mean(yg * yg, axis=-1, keepdims=True) + EPS)
    y = yg.reshape(bsz, s, SSD_INNER) * norm_g.astype(f32)
    return y.astype(p.dtype)


def causal_block_attention(q, k, v, scale):
    bsz, s, h, dk = q.shape
    nb = s // ATTN_BLOCK
    qb = q.reshape(bsz, nb, ATTN_BLOCK, h, dk).swapaxes(0, 1)
    kpos = jnp.arange(s)

    def one_block(args):
        qi, i = args
        sc = jnp.einsum('bqhd,bkhd->bhqk', qi, k, preferred_element_type=jnp.float32) * scale
        qpos = i * ATTN_BLOCK + jnp.arange(ATTN_BLOCK)
        sc = jnp.where(kpos[None, :] <= qpos[:, None], sc, -jnp.inf)
        pr = jax.nn.softmax(sc, axis=-1).astype(v.dtype)
        return jnp.einsum('bhqk,bkhd->bqhd', pr, v)

    out = lax.map(one_block, (qb, jnp.arange(nb)))
    return out.swapaxes(0, 1).reshape(bsz, s, h * v.shape[-1])


def mla_mixer(p, q_norm_g, w_uq, kv_norm_g, w_ukv, cos, sin):
    bsz, s, _ = p.shape
    cq, ckv, k_rope = jnp.split(p, [MLA_Q_RANK, MLA_Q_RANK + MLA_KV_RANK], axis=-1)
    qh = (rmsnorm(cq, q_norm_g) @ w_uq).reshape(bsz, s, MLA_HEADS, MLA_NOPE + MLA_ROPE)
    q_nope, q_rope = jnp.split(qh, [MLA_NOPE], axis=-1)
    q_rope = apply_rope(q_rope, cos, sin)
    kv = (rmsnorm(ckv, kv_norm_g) @ w_ukv).reshape(bsz, s, MLA_HEADS, MLA_NOPE + MLA_V)
    k_nope, v = jnp.split(kv, [MLA_NOPE], axis=-1)
    k_rope = apply_rope(k_rope[:, :, None, :], cos, sin)
    qf = jnp.concatenate([q_nope, q_rope], axis=-1)
    kf = jnp.concatenate([k_nope, jnp.broadcast_to(k_rope, (bsz, s, MLA_HEADS, MLA_ROPE))], axis=-1)
    return causal_block_attention(qf, kf, v, 1.0 / math.sqrt(MLA_NOPE + MLA_ROPE))


def swa_sink_mixer(p, sinks):
    bsz, s, _ = p.shape
    grp = SWA_HEADS // SWA_KV_HEADS
    nb = s // WINDOW
    qd = SWA_HEADS * SWA_HEAD_DIM
    kd = SWA_KV_HEADS * SWA_HEAD_DIM
    q, k, v = jnp.split(p, [qd, qd + kd], axis=-1)
    qb = q.reshape(bsz, nb, WINDOW, SWA_KV_HEADS, grp, SWA_HEAD_DIM)

    def band(t):
        tp = jnp.pad(t.reshape(bsz, s, SWA_KV_HEADS, SWA_HEAD_DIM), ((0, 0), (WINDOW, 0), (0, 0), (0, 0)))
        tp = tp.reshape(bsz, nb + 1, WINDOW, SWA_KV_HEADS, SWA_HEAD_DIM)
        return jnp.concatenate([tp[:, :-1], tp[:, 1:]], axis=2)

    kb, vb = band(k), band(v)
    sc = jnp.einsum('bnqhgd,bnkhd->bnhgqk', qb, kb, preferred_element_type=jnp.float32)
    sc = sc * (1.0 / math.sqrt(SWA_HEAD_DIM))
    i = jnp.arange(WINDOW)[:, None]
    j = jnp.arange(2 * WINDOW)[None, :]
    rel = WINDOW + i - j
    blk = jnp.arange(nb)[:, None, None]
    valid = (rel >= 0) & (rel < WINDOW) & ((blk - 1) * WINDOW + j >= 0)
    sc = jnp.where(valid[None, :, None, None], sc, -jnp.inf)
    sink = jnp.broadcast_to(sinks.astype(jnp.float32).reshape(1, 1, SWA_KV_HEADS, grp, 1, 1),
                            sc.shape[:-1] + (1,))
    pr = jax.nn.softmax(jnp.concatenate([sc, sink], axis=-1), axis=-1)[..., :-1]
    o = jnp.einsum('bnhgqk,bnkhd->bnqhgd', pr.astype(vb.dtype), vb)
    return o.reshape(bsz, s, qd)


def setup_inputs(seed: int = 0) -> dict:
    key = jax.random.key(seed)
    ks = iter(jax.random.split(key, 32))
    f32 = jnp.float32
    L = DEPTH

    def nrm(shape, scale):
        return jax.random.normal(next(ks), shape, f32) * scale

    def gain(shape):
        return 1.0 + nrm(shape, 0.02)

    x = nrm((BATCH, SEQ, D_MODEL), 1.0)
    c = nrm((BATCH, D_MODEL), 1.0)
    offs = jax.random.randint(next(ks), (BATCH, 1), 0, MAX_POS_OFFSET, dtype=jnp.int32)
    positions = offs + jnp.arange(SEQ, dtype=jnp.int32)[None, :]
    ada_w = nrm((L, D_MODEL, 6 * D_MODEL), 0.5 * D_MODEL ** -0.5)
    ada_b = nrm((L, 6 * D_MODEL), 0.02)
    norm1_g = gain((L, D_MODEL))
    norm2_g = gain((L, D_MODEL))
    w_in = nrm((L, D_MODEL, D_IN), D_MODEL ** -0.5)
    ssd_conv_w = nrm((L, SSD_CONV, SSD_XBC), SSD_CONV ** -0.5)
    ssd_conv_b = nrm((L, SSD_XBC), 0.02)
    dt0 = jnp.exp(jax.random.uniform(next(ks), (L, SSD_HEADS), f32)
                  * (math.log(DT_MAX) - math.log(DT_MIN)) + math.log(DT_MIN))
    ssd_dt_bias = dt0 + jnp.log(-jnp.expm1(-dt0))
    ssd_a_log = jnp.log(jax.random.uniform(next(ks), (L, SSD_HEADS), f32, 1.0, 16.0))
    ssd_d = 1.0 + nrm((L, SSD_HEADS), 0.1)
    ssd_norm_g = gain((L, SSD_INNER))
    mla_q_norm_g = gain((L, MLA_Q_RANK))
    mla_w_uq = nrm((L, MLA_Q_RANK, MLA_HEADS * (MLA_NOPE + MLA_ROPE)), MLA_Q_RANK ** -0.5)
    mla_kv_norm_g = gain((L, MLA_KV_RANK))
    mla_w_ukv = nrm((L, MLA_KV_RANK, MLA_HEADS * (MLA_NOPE + MLA_V)), MLA_KV_RANK ** -0.5)
    swa_sinks = nrm((L, SWA_HEADS), 1.0)
    w_out = nrm((L, D_MIX, D_MODEL), D_MIX ** -0.5)
    ffn_w_up = nrm((L, D_MODEL, 2 * D_FF), D_MODEL ** -0.5)
    ffn_conv_w = nrm((L, FFN_CONV, 2 * D_FF), FFN_CONV ** -0.5)
    ffn_conv_b = nrm((L, 2 * D_FF), 0.02)
    ffn_w_down = nrm((L, D_FF, D_MODEL), D_FF ** -0.5)
    final_norm_g = gain((D_MODEL,))
    return {"x": x, "c": c, "positions": positions,
            "ada_w": ada_w, "ada_b": ada_b, "norm1_g": norm1_g, "norm2_g": norm2_g,
            "w_in": w_in, "ssd_conv_w": ssd_conv_w, "ssd_conv_b": ssd_conv_b,
            "ssd_dt_bias": ssd_dt_bias, "ssd_a_log": ssd_a_log, "ssd_d": ssd_d,
            "ssd_norm_g": ssd_norm_g, "mla_q_norm_g": mla_q_norm_g, "mla_w_uq": mla_w_uq,
            "mla_kv_norm_g": mla_kv_norm_g, "mla_w_ukv": mla_w_ukv, "swa_sinks": swa_sinks,
            "w_out": w_out, "ffn_w_up": ffn_w_up, "ffn_conv_w": ffn_conv_w,
            "ffn_conv_b": ffn_conv_b, "ffn_w_down": ffn_w_down, "final_norm_g": final_norm_g}


def reference(x, c, positions, ada_w, ada_b, norm1_g, norm2_g, w_in, ssd_conv_w, ssd_conv_b,
              ssd_dt_bias, ssd_a_log, ssd_d, ssd_norm_g, mla_q_norm_g, mla_w_uq, mla_kv_norm_g,
              mla_w_ukv, swa_sinks, w_out, ffn_w_up, ffn_conv_w, ffn_conv_b, ffn_w_down,
              final_norm_g):
    inv_freq = ROPE_THETA ** (-jnp.arange(0, MLA_ROPE, 2, dtype=jnp.float32) / MLA_ROPE)
    ang = positions.astype(jnp.float32)[..., None] * inv_freq
    cos = jnp.cos(ang)[:, :, None, :]
    sin = jnp.sin(ang)[:, :, None, :]
    c_act = jax.nn.silu(c)
    for l in range(DEPTH):
        mod = c_act @ ada_w[l] + ada_b[l]
        sh1, sc1, g1, sh2, sc2, g2 = [m[:, None, :] for m in jnp.split(mod, 6, axis=-1)]
        h = rmsnorm(x, norm1_g[l]) * (1.0 + sc1) + sh1
        proj = h @ w_in[l]
        p_ssd, p_mla, p_swa = jnp.split(proj, [SSD_IN, SSD_IN + MLA_IN], axis=-1)
        y_ssd = ssd_mixer(p_ssd, ssd_conv_w[l], ssd_conv_b[l], ssd_dt_bias[l], ssd_a_log[l],
                          ssd_d[l], ssd_norm_g[l])
        y_mla = mla_mixer(p_mla, mla_q_norm_g[l], mla_w_uq[l], mla_kv_norm_g[l], mla_w_ukv[l],
                          cos, sin)
        y_swa = swa_sink_mixer(p_swa, swa_sinks[l])
        y = jnp.concatenate([y_ssd, y_mla, y_swa], axis=-1) @ w_out[l]
        x = x + g1 * y
        h = rmsnorm(x, norm2_g[l]) * (1.0 + sc2) + sh2
        u = causal_dwconv(h @ ffn_w_up[l], ffn_conv_w[l], ffn_conv_b[l])
        a, b = jnp.split(u, 2, axis=-1)
        x = x + g2 * ((jax.nn.silu(a) * b) @ ffn_w_down[l])
    return rmsnorm(x, final_norm_g)
```

```python
import functools
import math

import jax
import jax.numpy as jnp
from jax import lax
from jax.experimental import pallas as pl
from jax.experimental.pallas import tpu as pltpu

F32 = jnp.float32
BF16 = jnp.bfloat16

D_MODEL = 1024
EPS = 1e-6
SSD_HEADS = 8
SSD_HEAD_DIM = 64
SSD_INNER = SSD_HEADS * SSD_HEAD_DIM
SSD_GROUPS = 2
SSD_STATE = 128
SSD_CONV = 4
SSD_CHUNK = 128
SSD_XBC = SSD_INNER + 2 * SSD_GROUPS * SSD_STATE
MLA_HEADS = 4
MLA_NOPE = 64
MLA_ROPE = 32
MLA_V = 64
MLA_Q_RANK = 256
MLA_KV_RANK = 128
ROPE_THETA = 10000.0
SWA_HEADS = 4
SWA_KV_HEADS = 2
SWA_HEAD_DIM = 64
WINDOW = 128
D_FF = 2816
FFN_CONV = 3
SSD_IN = SSD_INNER + SSD_XBC + SSD_HEADS
MLA_IN = MLA_Q_RANK + MLA_KV_RANK + MLA_ROPE
SWA_IN = (SWA_HEADS + 2 * SWA_KV_HEADS) * SWA_HEAD_DIM
D_IN = SSD_IN + MLA_IN + SWA_IN

LANES = 128
SUBLANES = 8
HALF = LANES // 2
VMEM_LIMIT = 56 * 1024 * 1024

C_Z = 0
C_XBC = C_Z + SSD_INNER
C_CQ = C_XBC + SSD_XBC
C_CKV = C_CQ + MLA_Q_RANK
C_KR = C_CKV + MLA_KV_RANK
C_SQ = C_KR + LANES
C_SK = C_SQ + SWA_HEADS * SWA_HEAD_DIM
C_SV = C_SK + SWA_KV_HEADS * SWA_HEAD_DIM
C_DT = C_SV + SWA_KV_HEADS * SWA_HEAD_DIM
D_IN_PACKED = C_DT + LANES

FF_TILE = 256
FF_CHUNKS = D_FF // FF_TILE


def _params(semantics):
    return pltpu.CompilerParams(dimension_semantics=semantics, vmem_limit_bytes=VMEM_LIMIT)


def _silu(v):
    return v * jax.nn.sigmoid(v)


def _norm_mod(x, g, sc, sh):
    ms = jnp.mean(x * x, axis=-1, keepdims=True)
    y = x * lax.rsqrt(ms + EPS)
    return (y * g) * (1.0 + sc) + sh


def _rms(x, g):
    ms = jnp.mean(x * x, axis=-1, keepdims=True)
    return (x * lax.rsqrt(ms + EPS)) * g


def _mod_parts(m):
    d = D_MODEL
    return [m[:, k * d:(k + 1) * d] for k in range(6)]


def _ada_kernel(c_ref, w_ref, b_ref, o_ref):
    ca = _silu(c_ref[...]).astype(BF16)
    o_ref[0] = jnp.dot(ca, w_ref[0].astype(BF16), preferred_element_type=F32) + b_ref[0]


def _ada_call(c, ada_w, ada_b):
    nl, d, n6 = ada_w.shape
    bsz = c.shape[0]
    tn = 1536
    return pl.pallas_call(
        _ada_kernel,
        grid=(nl, n6 // tn),
        in_specs=[
            pl.BlockSpec((bsz, d), lambda l, j: (0, 0)),
            pl.BlockSpec((1, d, tn), lambda l, j: (l, 0, j)),
            pl.BlockSpec((1, 1, tn), lambda l, j: (l, 0, j)),
        ],
        out_specs=pl.BlockSpec((1, bsz, tn), lambda l, j: (l, 0, j)),
        out_shape=jax.ShapeDtypeStruct((nl, bsz, n6), F32),
        compiler_params=_params(("parallel", "parallel")),
        name="ada_mod",
    )(c, ada_w, ada_b.reshape(nl, 1, n6))


def _rope_kernel(pos_ref, f_ref, sg_ref, ct_ref, st_ref):
    ang = pos_ref[0] * f_ref[...]
    ct_ref[0] = jnp.cos(ang)
    st_ref[0] = jnp.sin(ang) * sg_ref[...]


def _rope_call(positions):
    bsz, s = positions.shape
    ts = min(s, 1024)
    half = MLA_ROPE // 2
    inv_freq = ROPE_THETA ** (-jnp.arange(0, MLA_ROPE, 2, dtype=F32) / MLA_ROPE)
    zeros = jnp.zeros((MLA_NOPE,), F32)
    pad = jnp.zeros((LANES - MLA_NOPE - MLA_ROPE,), F32)
    freq = jnp.concatenate([zeros, inv_freq, inv_freq, pad]).reshape(1, LANES)
    sign = jnp.concatenate([zeros, -jnp.ones((half,), F32), jnp.ones((half,), F32), pad]).reshape(1, LANES)
    pos = positions.astype(F32).reshape(bsz, s, 1)
    return pl.pallas_call(
        _rope_kernel,
        grid=(bsz, s // ts),
        in_specs=[
            pl.BlockSpec((1, ts, 1), lambda b, i: (b, i, 0)),
            pl.BlockSpec((1, LANES), lambda b, i: (0, 0)),
            pl.BlockSpec((1, LANES), lambda b, i: (0, 0)),
        ],
        out_specs=[pl.BlockSpec((1, ts, LANES), lambda b, i: (b, i, 0))] * 2,
        out_shape=[jax.ShapeDtypeStruct((bsz, s, LANES), F32)] * 2,
        compiler_params=_params(("parallel", "parallel")),
        name="rope_tables",
    )(pos, freq, sign)


def _apply_rope(v, ct, st):
    lane = lax.broadcasted_iota(jnp.int32, v.shape, 1)
    first_half = lane < MLA_NOPE + MLA_ROPE // 2
    partner = jnp.where(first_half,
                        pltpu.roll(v, LANES - MLA_ROPE // 2, 1),
                        pltpu.roll(v, MLA_ROPE // 2, 1))
    return v * ct + partner * st


def _prenorm_kernel(x_ref, mod_ref, g_ref, h_ref):
    sh1, sc1 = _mod_parts(mod_ref[0])[:2]
    h_ref[0] = _norm_mod(x_ref[0], g_ref[0], sc1, sh1).astype(BF16)


def _prenorm_call(x, mod3, norm_g, l, tm):
    bsz, s, d = x.shape
    return pl.pallas_call(
        _prenorm_kernel,
        grid=(bsz, s // tm),
        in_specs=[
            pl.BlockSpec((1, tm, d), lambda b, i: (b, i, 0)),
            pl.BlockSpec((1, 1, 6 * d), lambda b, i: (l * bsz + b, 0, 0)),
            pl.BlockSpec((1, 1, d), lambda b, i: (l, 0, 0)),
        ],
        out_specs=pl.BlockSpec((1, tm, d), lambda b, i: (b, i, 0)),
        out_shape=jax.ShapeDtypeStruct((bsz, s, d), BF16),
        compiler_params=_params(("parallel", "parallel")),
        name="prenorm",
    )(x, mod3, norm_g)


def _inproj_kernel(h_ref, w_ref, qg_ref, wuq_ref, kvg_ref, wuk_ref, wuv_ref, ct_ref, st_ref, dtb_ref,
                   z_ref, xbc_ref, dt_ref, qm_ref, km_ref, vm_ref, sq_ref, sk_ref, sv_ref, *, tm):
    h = h_ref[0]

    def proj(c0, width):
        return jnp.dot(h, w_ref[0, :, c0:c0 + width], preferred_element_type=F32)

    z_ref[0] = proj(C_Z, SSD_INNER)
    xbc_ref[0] = proj(C_XBC, SSD_XBC)
    sq_ref[0] = proj(C_SQ, SWA_HEADS * SWA_HEAD_DIM).astype(BF16)
    sk_ref[0] = proj(C_SK, SWA_KV_HEADS * SWA_HEAD_DIM).astype(BF16)
    sv_ref[0] = proj(C_SV, SWA_KV_HEADS * SWA_HEAD_DIM).astype(BF16)

    dt_t = proj(C_DT, LANES).T[0:SSD_HEADS, :] + dtb_ref[0]
    dt_sp = jnp.maximum(dt_t, 0.0) + jnp.log1p(jnp.exp(-jnp.abs(dt_t)))
    for c in range(tm // SSD_CHUNK):
        dt_ref[0, c] = dt_sp[:, c * SSD_CHUNK:(c + 1) * SSD_CHUNK]

    ct = ct_ref[0]
    st = st_ref[0]
    cq = _rms(proj(C_CQ, MLA_Q_RANK), qg_ref[0]).astype(BF16)
    ckv = _rms(proj(C_CKV, MLA_KV_RANK), kvg_ref[0]).astype(BF16)
    k_rope = _apply_rope(proj(C_KR, LANES), ct, st)
    for hd in range(MLA_HEADS):
        lo, hi = hd * LANES, (hd + 1) * LANES
        qh = jnp.dot(cq, wuq_ref[0, :, lo:hi], preferred_element_type=F32)
        qm_ref[0, :, lo:hi] = _apply_rope(qh, ct, st).astype(BF16)
        kh = jnp.dot(ckv, wuk_ref[0, :, lo:hi], preferred_element_type=F32)
        km_ref[0, :, lo:hi] = (kh + k_rope).astype(BF16)
    vm_ref[0] = jnp.dot(ckv, wuv_ref[0], preferred_element_type=F32).astype(BF16)


def _inproj_call(h, wp, ct, st, l, tm):
    bsz, s, d = h.shape
    nq = s // SSD_CHUNK
    full = lambda shape: pl.BlockSpec((1,) + shape, lambda b, i: (l,) + (0,) * len(shape))
    tok = lambda width: pl.BlockSpec((1, tm, width), lambda b, i: (b, i, 0))
    kern = functools.partial(_inproj_kernel, tm=tm)
    return pl.pallas_call(
        kern,
        grid=(bsz, s // tm),
        in_specs=[
            tok(d),
            full((d, D_IN_PACKED)),
            full((1, MLA_Q_RANK)),
            full((MLA_Q_RANK, MLA_HEADS * LANES)),
            full((1, MLA_KV_RANK)),
            full((MLA_KV_RANK, MLA_HEADS * LANES)),
            full((MLA_KV_RANK, MLA_HEADS * MLA_V)),
            tok(LANES),
            tok(LANES),
            full((SSD_HEADS, 1)),
        ],
        out_specs=[
            tok(SSD_INNER),
            tok(SSD_XBC),
            pl.BlockSpec((1, tm // SSD_CHUNK, SSD_HEADS, SSD_CHUNK), lambda b, i: (b, i, 0, 0)),
            tok(MLA_HEADS * LANES),
            tok(MLA_HEADS * LANES),
            tok(MLA_HEADS * MLA_V),
            tok(SWA_HEADS * SWA_HEAD_DIM),
            tok(SWA_KV_HEADS * SWA_HEAD_DIM),
            tok(SWA_KV_HEADS * SWA_HEAD_DIM),
        ],
        out_shape=[
            jax.ShapeDtypeStruct((bsz, s, SSD_INNER), F32),
            jax.ShapeDtypeStruct((bsz, s, SSD_XBC), F32),
            jax.ShapeDtypeStruct((bsz, nq, SSD_HEADS, SSD_CHUNK), F32),
            jax.ShapeDtypeStruct((bsz, s, MLA_HEADS * LANES), BF16),
            jax.ShapeDtypeStruct((bsz, s, MLA_HEADS * LANES), BF16),
            jax.ShapeDtypeStruct((bsz, s, MLA_HEADS * MLA_V), BF16),
            jax.ShapeDtypeStruct((bsz, s, SWA_HEADS * SWA_HEAD_DIM), BF16),
            jax.ShapeDtypeStruct((bsz, s, SWA_KV_HEADS * SWA_HEAD_DIM), BF16),
            jax.ShapeDtypeStruct((bsz, s, SWA_KV_HEADS * SWA_HEAD_DIM), BF16),
        ],
        compiler_params=_params(("parallel", "parallel")),
        name="in_proj",
    )(h, wp["w_in"], wp["q_norm_g"], wp["w_uq"], wp["kv_norm_g"], wp["w_uk"], wp["w_uv"], ct, st,
      wp["dt_bias"])


def _cumsum_lanes(v):
    lane = lax.broadcasted_iota(jnp.int32, v.shape, 1)
    d = 1
    while d < v.shape[1]:
        v = v + jnp.where(lane >= d, pltpu.roll(v, d, 1), 0.0)
        d *= 2
    return v


def _ssd_kernel(z_ref, xbc_ref, dt_ref, cw_ref, cb_ref, alog_ref, dsk_ref, ng_ref, y_ref,
                xpad_ref, xc_ref, h_ref, *, ts):
    q = SSD_CHUNK
    n = SSD_STATE
    halo = SUBLANES
    i = pl.program_id(1)

    @pl.when(i == 0)
    def _():
        h_ref[...] = jnp.zeros_like(h_ref)
        xpad_ref[0:halo, :] = jnp.zeros((halo, SSD_XBC), F32)

    xpad_ref[halo:halo + ts, :] = xbc_ref[0]
    cw = cw_ref[0]
    xc = cb_ref[0]
    for k in range(SSD_CONV):
        off = halo - (SSD_CONV - 1) + k
        xc = xc + xpad_ref[off:off + ts, :] * cw[k:k + 1, :]
    xc_ref[...] = _silu(xc)
    xpad_ref[0:halo, :] = xpad_ref[ts:ts + halo, :]

    a_neg = -jnp.exp(alog_ref[0])
    row = lax.broadcasted_iota(jnp.int32, (q, q), 0)
    col = lax.broadcasted_iota(jnp.int32, (q, q), 1)
    causal = col <= row
    low_half = lax.broadcasted_iota(jnp.int32, (q, LANES), 1) < HALF
    low_half_n = lax.broadcasted_iota(jnp.int32, (n, LANES), 1) < HALF
    low_half_1 = lax.broadcasted_iota(jnp.int32, (1, LANES), 1) < HALF
    heads_per_group = SSD_HEADS // SSD_GROUPS

    def chunk(c, carry):
        r0 = pl.multiple_of(c * q, q)
        dt = dt_ref[0, c]
        a_rows = _cumsum_lanes(dt * a_neg)
        ss = [None] * SSD_GROUPS
        y_pairs = [None] * (SSD_HEADS // 2)
        for g in range(SSD_GROUPS):
            b_g = xc_ref[pl.ds(r0, q), SSD_INNER + g * n:SSD_INNER + (g + 1) * n]
            c_g = xc_ref[pl.ds(r0, q), SSD_INNER + (SSD_GROUPS + g) * n:SSD_INNER + (SSD_GROUPS + g + 1) * n]
            cb = lax.dot_general(c_g.astype(BF16), b_g.astype(BF16), (((1,), (1,)), ((), ())),
                                 preferred_element_type=F32)
            b_t = b_g.T
            for jj in range(heads_per_group // 2):
                j = g * (heads_per_group // 2) + jj
                lanes = slice(j * LANES, (j + 1) * LANES)
                x_pair = xc_ref[pl.ds(r0, q), lanes]
                x_pair_bf = x_pair.astype(BF16)
                h_pair = h_ref[:, lanes]
                h_pair_bf = h_pair.astype(BF16)
                ys = []
                ds = []
                cds = []
                for hh in (2 * j, 2 * j + 1):
                    a_row = a_rows[hh:hh + 1, :]
                    dt_row = dt[hh:hh + 1, :]
                    row_bc = jnp.broadcast_to(a_row, (q, q))
                    col_bc = row_bc.T
                    decay = jnp.exp(jnp.where(causal, col_bc - row_bc, -jnp.inf))
                    m = (cb * decay) * dt_row
                    c_scaled = c_g * jnp.exp(col_bc)
                    y_h = (jnp.dot(m.astype(BF16), x_pair_bf, preferred_element_type=F32)
                           + jnp.dot(c_scaled.astype(BF16), h_pair_bf, preferred_element_type=F32))
                    a_last = a_row[:, q - 1:q]
                    w_row = jnp.exp(a_last - a_row) * dt_row
                    d_state = jnp.dot((b_t * w_row).astype(BF16), x_pair_bf, preferred_element_type=F32)
                    ys.append(y_h)
                    ds.append(d_state)
                    cds.append(jnp.exp(a_last))
                h_ref[:, lanes] = (h_pair * jnp.where(low_half_1, cds[0], cds[1])
                                   + jnp.where(low_half_n, ds[0], ds[1]))
                y_pair = jnp.where(low_half, ys[0], ys[1]) + x_pair * dsk_ref[0, :, lanes]
                y_pair = y_pair * _silu(z_ref[0, pl.ds(r0, q), lanes])
                y_pairs[j] = y_pair
                sq = jnp.sum(y_pair * y_pair, axis=-1, keepdims=True)
                ss[g] = sq if ss[g] is None else ss[g] + sq
        group_width = SSD_INNER // SSD_GROUPS
        for j in range(SSD_HEADS // 2):
            g = j // (heads_per_group // 2)
            lanes = slice(j * LANES, (j + 1) * LANES)
            inv = lax.rsqrt(ss[g] / group_width + EPS)
            y_ref[0, pl.ds(r0, q), lanes] = ((y_pairs[j] * inv) * ng_ref[0, :, lanes]).astype(BF16)
        return carry

    lax.fori_loop(0, ts // q, chunk, 0)


def _ssd_call(z, xbc, dt, wp, l, ts):
    bsz, s, _ = z.shape
    full = lambda shape: pl.BlockSpec((1,) + shape, lambda b, i: (l,) + (0,) * len(shape))
    tok = lambda width: pl.BlockSpec((1, ts, width), lambda b, i: (b, i, 0))
    kern = functools.partial(_ssd_kernel, ts=ts)
    return pl.pallas_call(
        kern,
        grid=(bsz, s // ts),
        in_specs=[
            tok(SSD_INNER),
            tok(SSD_XBC),
            pl.BlockSpec((1, ts // SSD_CHUNK, SSD_HEADS, SSD_CHUNK), lambda b, i: (b, i, 0, 0)),
            full((SSD_CONV, SSD_XBC)),
            full((1, SSD_XBC)),
            full((SSD_HEADS, 1)),
            full((1, SSD_INNER)),
            full((1, SSD_INNER)),
        ],
        out_specs=tok(SSD_INNER),
        out_shape=jax.ShapeDtypeStruct((bsz, s, SSD_INNER), BF16),
        scratch_shapes=[
            pltpu.VMEM((ts + SUBLANES, SSD_XBC), F32),
            pltpu.VMEM((ts, SSD_XBC), F32),
            pltpu.VMEM((SSD_STATE, SSD_INNER), F32),
        ],
        compiler_params=_params(("parallel", "arbitrary")),
        name="ssd_mixer",
    )(z, xbc, dt, wp["ssd_conv_w"], wp["ssd_conv_b"], wp["ssd_a_log"], wp["ssd_d_lanes"], wp["ssd_norm_g"])


def _mla_kernel(q_ref, k_ref, v_ref, o_ref, m_ref, l_ref, acc_ref, *, tq, scale):
    qi = pl.program_id(1)
    row = lax.broadcasted_iota(jnp.int32, (tq, tq), 0)
    col = lax.broadcasted_iota(jnp.int32, (tq, tq), 1)
    causal = col <= row
    low_half = lax.broadcasted_iota(jnp.int32, (tq, LANES), 1) < HALF
    outs = []
    for hd in range(MLA_HEADS):
        k_lanes = slice(hd * LANES, (hd + 1) * LANES)
        v_lanes = slice((hd // 2) * LANES, (hd // 2 + 1) * LANES)
        q = q_ref[0, :, k_lanes]
        m_ref[...] = jnp.full(m_ref.shape, -jnp.inf, F32)
        l_ref[...] = jnp.zeros(l_ref.shape, F32)
        acc_ref[...] = jnp.zeros(acc_ref.shape, F32)

        def step(j, masked):
            r0 = pl.multiple_of(j * tq, tq)
            k = k_ref[0, pl.ds(r0, tq), k_lanes]
            v = v_ref[0, pl.ds(r0, tq), v_lanes]
            s = lax.dot_general(q, k, (((1,), (1,)), ((), ())), preferred_element_type=F32) * scale
            if masked:
                s = jnp.where(causal, s, -jnp.inf)
            m_prev = m_ref[...]
            m_new = jnp.maximum(m_prev, jnp.max(s, axis=-1, keepdims=True))
            alpha = jnp.exp(m_prev - m_new)
            p = jnp.exp(s - m_new)
            l_ref[...] = alpha * l_ref[...] + jnp.sum(p, axis=-1, keepdims=True)
            acc_ref[...] = alpha * acc_ref[...] + jnp.dot(p.astype(BF16), v, preferred_element_type=F32)
            m_ref[...] = m_new

        def body(j, carry):
            step(j, False)
            return carry

        lax.fori_loop(0, qi, body, 0)
        step(qi, True)
        outs.append(acc_ref[...] / l_ref[...])
    for j in range(MLA_HEADS // 2):
        o_ref[0, :, j * LANES:(j + 1) * LANES] = jnp.where(low_half, outs[2 * j], outs[2 * j + 1]).astype(BF16)


def _mla_call(qm, km, vm, tq):
    bsz, s, _ = qm.shape
    kern = functools.partial(_mla_kernel, tq=tq, scale=1.0 / math.sqrt(MLA_NOPE + MLA_ROPE))
    return pl.pallas_call(
        kern,
        grid=(bsz, s // tq),
        in_specs=[
            pl.BlockSpec((1, tq, MLA_HEADS * LANES), lambda b, i: (b, i, 0)),
            pl.BlockSpec((1, s, MLA_HEADS * LANES), lambda b, i: (b, 0, 0)),
            pl.BlockSpec((1, s, MLA_HEADS * MLA_V), lambda b, i: (b, 0, 0)),
        ],
        out_specs=pl.BlockSpec((1, tq, MLA_HEADS * MLA_V), lambda b, i: (b, i, 0)),
        out_shape=jax.ShapeDtypeStruct((bsz, s, MLA_HEADS * MLA_V), BF16),
        scratch_shapes=[
            pltpu.VMEM((tq, 1), F32),
            pltpu.VMEM((tq, 1), F32),
            pltpu.VMEM((tq, LANES), F32),
        ],
        compiler_params=_params(("parallel", "parallel")),
        name="mla_attention",
    )(qm, km, vm)


def _swa_kernel(sink_ref, q_ref, k_ref, v_ref, kp_ref, vp_ref, o_ref, *, ts, l):
    w = WINDOW
    i = pl.program_id(1)
    row = lax.broadcasted_iota(jnp.int32, (w, w), 0)
    col = lax.broadcasted_iota(jnp.int32, (w, w), 1)
    cur_ok = col <= row
    prev_ok = col > row
    lane = lax.broadcasted_iota(jnp.int32, (w, LANES), 1)
    low_half = lane < HALF
    scale = 1.0 / math.sqrt(SWA_HEAD_DIM)
    nt = (((1,), (1,)), ((), ()))
    for blk in range(ts // w):
        r = slice(blk * w, (blk + 1) * w)
        k_cur = k_ref[0, r, :]
        v_cur = v_ref[0, r, :]
        if blk == 0:
            k_prev = kp_ref[0]
            v_prev = vp_ref[0]
            first_key = jnp.where(i > 0, 0, w)
        else:
            rp = slice((blk - 1) * w, blk * w)
            k_prev = k_ref[0, rp, :]
            v_prev = v_ref[0, rp, :]
            first_key = 0
        for cgrp in range(2):
            q_pair = q_ref[0, r, cgrp * LANES:(cgrp + 1) * LANES]
            halves = []
            for kvh in range(SWA_KV_HEADS):
                head = cgrp + 2 * kvh
                kv_sel = low_half if kvh == 0 else jnp.logical_not(low_half)
                qh = jnp.where(kv_sel, q_pair, jnp.zeros_like(q_pair))
                s_cur = lax.dot_general(qh, k_cur, nt, preferred_element_type=F32) * scale
                s_prev = lax.dot_general(qh, k_prev, nt, preferred_element_type=F32) * scale
                ok_prev = jnp.logical_and(prev_ok, col >= first_key)
                s_cur = jnp.where(cur_ok, s_cur, -jnp.inf)
                s_prev = jnp.where(ok_prev, s_prev, -jnp.inf)
                sink = sink_ref[l, head]
                m = jnp.maximum(jnp.maximum(jnp.max(s_cur, axis=-1, keepdims=True),
                                            jnp.max(s_prev, axis=-1, keepdims=True)), sink)
                p_cur = jnp.exp(s_cur - m)
                p_prev = jnp.exp(s_prev - m)
                denom = (jnp.sum(p_cur, axis=-1, keepdims=True) + jnp.sum(p_prev, axis=-1, keepdims=True)
                         + jnp.exp(sink - m))
                pv = (jnp.dot(p_prev.astype(BF16), v_prev, preferred_element_type=F32)
                      + jnp.dot(p_cur.astype(BF16), v_cur, preferred_element_type=F32))
                halves.append(pv / denom)
            o_ref[0, r, cgrp * LANES:(cgrp + 1) * LANES] = jnp.where(low_half, halves[0], halves[1]).astype(BF16)


def _swa_call(sq, sk, sv, sinks, l, ts):
    bsz, s, _ = sq.shape
    w = WINDOW
    kvw = SWA_KV_HEADS * SWA_HEAD_DIM
    per = ts // w
    kern = functools.partial(_swa_kernel, ts=ts, l=l)
    prev_map = lambda b, i: (b, jnp.maximum(i * per - 1, 0), 0)
    return pl.pallas_call(
        kern,
        grid=(bsz, s // ts),
        in_specs=[
            pl.BlockSpec(memory_space=pltpu.SMEM),
            pl.BlockSpec((1, ts, SWA_HEADS * SWA_HEAD_DIM), lambda b, i: (b, i, 0)),
            pl.BlockSpec((1, ts, kvw), lambda b, i: (b, i, 0)),
            pl.BlockSpec((1, ts, kvw), lambda b, i: (b, i, 0)),
            pl.BlockSpec((1, w, kvw), prev_map),
            pl.BlockSpec((1, w, kvw), prev_map),
        ],
        out_specs=pl.BlockSpec((1, ts, SWA_HEADS * SWA_HEAD_DIM), lambda b, i: (b, i, 0)),
        out_shape=jax.ShapeDtypeStruct((bsz, s, SWA_HEADS * SWA_HEAD_DIM), BF16),
        compiler_params=_params(("parallel", "parallel")),
        name="swa_attention",
    )(sinks, sq, sk, sv, sk, sv)


def _outproj_kernel(ys_ref, ym_ref, yw_ref, x_ref, mod_ref, w_ref, g_ref, x1_ref, h2_ref):
    k1 = SSD_INNER
    k2 = k1 + MLA_HEADS * MLA_V
    y = (jnp.dot(ys_ref[0], w_ref[0, 0:k1, :], preferred_element_type=F32)
         + jnp.dot(ym_ref[0], w_ref[0, k1:k2, :], preferred_element_type=F32)
         + jnp.dot(yw_ref[0], w_ref[0, k2:, :], preferred_element_type=F32))
    _, _, g1, sh2, sc2, _ = _mod_parts(mod_ref[0])
    x1 = x_ref[0] + g1 * y
    x1_ref[0] = x1
    h2_ref[0] = _norm_mod(x1, g_ref[0], sc2, sh2).astype(BF16)


def _outproj_call(y_ssd, y_mla, y_swa, x, mod3, wp, l, tm):
    bsz, s, d = x.shape
    tok = lambda width: pl.BlockSpec((1, tm, width), lambda b, i: (b, i, 0))
    return pl.pallas_call(
        _outproj_kernel,
        grid=(bsz, s // tm),
        in_specs=[
            tok(SSD_INNER),
            tok(MLA_HEADS * MLA_V),
            tok(SWA_HEADS * SWA_HEAD_DIM),
            tok(d),
            pl.BlockSpec((1, 1, 6 * d), lambda b, i: (l * bsz + b, 0, 0)),
            pl.BlockSpec((1, d, d), lambda b, i: (l, 0, 0)),
            pl.BlockSpec((1, 1, d), lambda b, i: (l, 0, 0)),
        ],
        out_specs=[tok(d), tok(d)],
        out_shape=[jax.ShapeDtypeStruct((bsz, s, d), F32), jax.ShapeDtypeStruct((bsz, s, d), BF16)],
        compiler_params=_params(("parallel", "parallel")),
        name="out_proj",
    )(y_ssd, y_mla, y_swa, x, mod3, wp["w_out"], wp["norm2_g"])


def _ffn_kernel(h_ref, x_ref, mod_ref, modn_ref, wa_ref, wb_ref, cp_ref, wd_ref, gn_ref, *rest, tm, last):
    if last:
        x2_ref, ua_ref, ub_ref, ca_ref, cb_ref, acc_ref = rest
    else:
        x2_ref, hn_ref, ua_ref, ub_ref, ca_ref, cb_ref, acc_ref = rest
    halo = SUBLANES
    i = pl.program_id(1)

    @pl.when(i == 0)
    def _():
        ca_ref[...] = jnp.zeros_like(ca_ref)
        cb_ref[...] = jnp.zeros_like(cb_ref)

    h = h_ref[0]

    def conv(u, buf_ref, carry_ref, j, taps):
        buf_ref[0:halo, :] = carry_ref[j]
        buf_ref[halo:halo + tm, :] = u
        carry_ref[j] = u[tm - halo:tm, :]
        out = taps[FFN_CONV:FFN_CONV + 1, :]
        for k in range(FFN_CONV - 1):
            off = halo - (FFN_CONV - 1) + k
            out = out + buf_ref[off:off + tm, :] * taps[k:k + 1, :]
        return out + u * taps[FFN_CONV - 1:FFN_CONV, :]

    for j in range(FF_CHUNKS):
        ua = jnp.dot(h, wa_ref[0, j], preferred_element_type=F32)
        ub = jnp.dot(h, wb_ref[0, j], preferred_element_type=F32)
        a = conv(ua, ua_ref, ca_ref, j, cp_ref[0, 0, j])
        b = conv(ub, ub_ref, cb_ref, j, cp_ref[0, 1, j])
        act = (_silu(a) * b).astype(BF16)
        part = jnp.dot(act, wd_ref[0, j], preferred_element_type=F32)
        if j == 0:
            acc_ref[...] = part
        else:
            acc_ref[...] += part

    g2 = _mod_parts(mod_ref[0])[5]
    x2 = x_ref[0] + g2 * acc_ref[...]
    if last:
        x2_ref[0] = _rms(x2, gn_ref[0])
    else:
        x2_ref[0] = x2
        sh1, sc1 = _mod_parts(modn_ref[0])[:2]
        hn_ref[0] = _norm_mod(x2, gn_ref[0], sc1, sh1).astype(BF16)


def _ffn_call(h2, x1, mod3, wp, next_g, l, next_l, tm, last):
    bsz, s, d = x1.shape
    tok = lambda width: pl.BlockSpec((1, tm, width), lambda b, i: (b, i, 0))
    once = pl.Buffered(1)
    kern = functools.partial(_ffn_kernel, tm=tm, last=last)
    gl = 0 if last else next_l
    outs = pl.pallas_call(
        kern,
        grid=(bsz, s // tm),
        in_specs=[
            tok(d),
            tok(d),
            pl.BlockSpec((1, 1, 6 * d), lambda b, i: (l * bsz + b, 0, 0)),
            pl.BlockSpec((1, 1, 6 * d), lambda b, i: (next_l * bsz + b, 0, 0)),
            pl.BlockSpec((1, FF_CHUNKS, d, FF_TILE), lambda b, i: (l, 0, 0, 0), pipeline_mode=once),
            pl.BlockSpec((1, FF_CHUNKS, d, FF_TILE), lambda b, i: (l, 0, 0, 0), pipeline_mode=once),
            pl.BlockSpec((1, 2, FF_CHUNKS, FFN_CONV + 1, FF_TILE), lambda b, i: (l, 0, 0, 0, 0)),
            pl.BlockSpec((1, FF_CHUNKS, FF_TILE, d), lambda b, i: (l, 0, 0, 0), pipeline_mode=once),
            pl.BlockSpec((1, 1, d), lambda b, i: (gl, 0, 0)),
        ],
        out_specs=[tok(d)] if last else [tok(d), tok(d)],
        out_shape=([jax.ShapeDtypeStruct((bsz, s, d), F32)] if last else
                   [jax.ShapeDtypeStruct((bsz, s, d), F32), jax.ShapeDtypeStruct((bsz, s, d), BF16)]),
        scratch_shapes=[
            pltpu.VMEM((tm + SUBLANES, FF_TILE), F32),
            pltpu.VMEM((tm + SUBLANES, FF_TILE), F32),
            pltpu.VMEM((FF_CHUNKS, SUBLANES, FF_TILE), F32),
            pltpu.VMEM((FF_CHUNKS, SUBLANES, FF_TILE), F32),
            pltpu.VMEM((tm, d), F32),
        ],
        compiler_params=_params(("parallel", "arbitrary")),
        name="ffn",
    )(h2, x1, mod3, mod3, wp["ffn_wa"], wp["ffn_wb"], wp["ffn_conv"], wp["ffn_wd"], next_g)
    return (outs[0], None) if last else outs


def _pack_weights(norm1_g, norm2_g, w_in, ssd_conv_w, ssd_conv_b, ssd_dt_bias, ssd_a_log, ssd_d,
                  ssd_norm_g, mla_q_norm_g, mla_w_uq, mla_kv_norm_g, mla_w_ukv, w_out, ffn_w_up,
                  ffn_conv_w, ffn_conv_b, ffn_w_down):
    nl = w_in.shape[0]
    row = lambda v: v.reshape(nl, 1, v.shape[-1])

    o_dt = SSD_INNER + SSD_XBC
    o_cq = SSD_IN
    o_ckv = o_cq + MLA_Q_RANK
    o_kr = o_ckv + MLA_KV_RANK
    o_sq = SSD_IN + MLA_IN
    o_sk = o_sq + SWA_HEADS * SWA_HEAD_DIM
    o_sv = o_sk + SWA_KV_HEADS * SWA_HEAD_DIM
    hd = SWA_HEAD_DIM
    idx = []
    idx += list(range(0, SSD_INNER + SSD_XBC))
    idx += list(range(o_cq, o_cq + MLA_Q_RANK))
    idx += list(range(o_ckv, o_ckv + MLA_KV_RANK))
    idx += [-1] * MLA_NOPE + list(range(o_kr, o_kr + MLA_ROPE)) + [-1] * (LANES - MLA_NOPE - MLA_ROPE)
    for head in (0, 2, 1, 3):
        idx += list(range(o_sq + head * hd, o_sq + (head + 1) * hd))
    idx += list(range(o_sk, o_sk + SWA_KV_HEADS * hd))
    idx += list(range(o_sv, o_sv + SWA_KV_HEADS * hd))
    idx += list(range(o_dt, o_dt + SSD_HEADS)) + [-1] * (LANES - SSD_HEADS)
    assert len(idx) == D_IN_PACKED
    idx = jnp.asarray(idx, jnp.int32)
    w_in_p = jnp.where(idx >= 0, jnp.take(w_in, jnp.maximum(idx, 0), axis=2), 0.0).astype(BF16)

    dq = MLA_NOPE + MLA_ROPE
    w_uq = mla_w_uq.reshape(nl, MLA_Q_RANK, MLA_HEADS, dq)
    w_uq = jnp.pad(w_uq, ((0, 0), (0, 0), (0, 0), (0, LANES - dq))).reshape(nl, MLA_Q_RANK, MLA_HEADS * LANES)
    w_ukv = mla_w_ukv.reshape(nl, MLA_KV_RANK, MLA_HEADS, MLA_NOPE + MLA_V)
    w_uk = jnp.pad(w_ukv[..., :MLA_NOPE], ((0, 0), (0, 0), (0, 0), (0, LANES - MLA_NOPE)))
    w_uk = w_uk.reshape(nl, MLA_KV_RANK, MLA_HEADS * LANES)
    w_uv = w_ukv[..., MLA_NOPE:].reshape(nl, MLA_KV_RANK, MLA_HEADS * MLA_V)

    o_w = SSD_INNER + MLA_HEADS * MLA_V
    w_swa = w_out[:, o_w:, :].reshape(nl, SWA_HEADS, hd, D_MODEL)[:, jnp.asarray([0, 2, 1, 3])]
    w_out_p = jnp.concatenate([w_out[:, :o_w, :], w_swa.reshape(nl, SWA_HEADS * hd, D_MODEL)], axis=1)

    wa = ffn_w_up[:, :, :D_FF].reshape(nl, D_MODEL, FF_CHUNKS, FF_TILE).transpose(0, 2, 1, 3)
    wb = ffn_w_up[:, :, D_FF:].reshape(nl, D_MODEL, FF_CHUNKS, FF_TILE).transpose(0, 2, 1, 3)
    taps = jnp.concatenate([ffn_conv_w, ffn_conv_b[:, None, :]], axis=1)
    taps = taps.reshape(nl, FFN_CONV + 1, 2, FF_CHUNKS, FF_TILE).transpose(0, 2, 3, 1, 4)
    wd = ffn_w_down.reshape(nl, FF_CHUNKS, FF_TILE, D_MODEL)

    return {
        "norm1_g": row(norm1_g), "norm2_g": row(norm2_g),
        "w_in": w_in_p,
        "ssd_conv_w": ssd_conv_w, "ssd_conv_b": row(ssd_conv_b),
        "dt_bias": ssd_dt_bias.reshape(nl, SSD_HEADS, 1),
        "ssd_a_log": ssd_a_log.reshape(nl, SSD_HEADS, 1),
        "ssd_d_lanes": row(jnp.repeat(ssd_d, SSD_HEAD_DIM, axis=-1)),
        "ssd_norm_g": row(ssd_norm_g),
        "q_norm_g": row(mla_q_norm_g), "w_uq": w_uq.astype(BF16),
        "kv_norm_g": row(mla_kv_norm_g), "w_uk": w_uk.astype(BF16), "w_uv": w_uv.astype(BF16),
        "w_out": w_out_p.astype(BF16),
        "ffn_wa": wa.astype(BF16), "ffn_wb": wb.astype(BF16), "ffn_conv": taps, "ffn_wd": wd.astype(BF16),
    }


def _tile(s, want):
    t = min(s, want)
    assert s % t == 0
    return t


def kernel(x, c, positions, ada_w, ada_b, norm1_g, norm2_g, w_in, ssd_conv_w, ssd_conv_b, ssd_dt_bias,
           ssd_a_log, ssd_d, ssd_norm_g, mla_q_norm_g, mla_w_uq, mla_kv_norm_g, mla_w_ukv, swa_sinks,
           w_out, ffn_w_up, ffn_conv_w, ffn_conv_b, ffn_w_down, final_norm_g):
    bsz, s, d = x.shape
    nl = w_in.shape[0]
    assert d == D_MODEL and s % SSD_CHUNK == 0
    tm = _tile(s, 512)
    ts = _tile(s, 512)
    tq = _tile(s, 512)

    wp = _pack_weights(norm1_g, norm2_g, w_in, ssd_conv_w, ssd_conv_b, ssd_dt_bias, ssd_a_log, ssd_d,
                       ssd_norm_g, mla_q_norm_g, mla_w_uq, mla_kv_norm_g, mla_w_ukv, w_out, ffn_w_up,
                       ffn_conv_w, ffn_conv_b, ffn_w_down)
    final_g = final_norm_g.reshape(1, 1, d)
    mod3 = _ada_call(c, ada_w, ada_b).reshape(nl * bsz, 1, 6 * d)
    ct, st = _rope_call(positions)

    h = _prenorm_call(x, mod3, wp["norm1_g"], 0, tm)
    for l in range(nl):
        z, xbc, dt, qm, km, vm, sq, sk, sv = _inproj_call(h, wp, ct, st, l, tm)
        y_ssd = _ssd_call(z, xbc, dt, wp, l, ts)
        y_mla = _mla_call(qm, km, vm, tq)
        y_swa = _swa_call(sq, sk, sv, swa_sinks, l, ts)
        x1, h2 = _outproj_call(y_ssd, y_mla, y_swa, x, mod3, wp, l, tm)
        last = l == nl - 1
        next_l = l if last else l + 1
        next_g = final_g if last else wp["norm1_g"]
        x, h = _ffn_call(h2, x1, mod3, wp, next_g, l, next_l, tm, last)
    return x
```

```python
import functools
import math

import jax
import jax.numpy as jnp
from jax import lax
from jax.experimental import pallas as pl
from jax.experimental.pallas import tpu as pltpu

F32 = jnp.float32
BF16 = jnp.bfloat16

D_MODEL = 1024
EPS = 1e-6
SSD_HEADS = 8
SSD_HEAD_DIM = 64
SSD_INNER = SSD_HEADS * SSD_HEAD_DIM
SSD_GROUPS = 2
SSD_STATE = 128
SSD_CONV = 4
SSD_CHUNK = 128
SSD_XBC = SSD_INNER + 2 * SSD_GROUPS * SSD_STATE
MLA_HEADS = 4
MLA_NOPE = 64
MLA_ROPE = 32
MLA_V = 64
MLA_Q_RANK = 256
MLA_KV_RANK = 128
ROPE_THETA = 10000.0
SWA_HEADS = 4
SWA_KV_HEADS = 2
SWA_HEAD_DIM = 64
WINDOW = 128
D_FF = 2816
FFN_CONV = 3
SSD_IN = SSD_INNER + SSD_XBC + SSD_HEADS
MLA_IN = MLA_Q_RANK + MLA_KV_RANK + MLA_ROPE
SWA_IN = (SWA_HEADS + 2 * SWA_KV_HEADS) * SWA_HEAD_DIM
D_IN = SSD_IN + MLA_IN + SWA_IN

LANES = 128
SUBLANES = 8
HALF = LANES // 2
VMEM_LIMIT = 56 * 1024 * 1024

C_Z = 0
C_XBC = C_Z + SSD_INNER
C_CQ = C_XBC + SSD_XBC
C_CKV = C_CQ + MLA_Q_RANK
C_KR = C_CKV + MLA_KV_RANK
C_SQ = C_KR + LANES
C_SK = C_SQ + SWA_HEADS * SWA_HEAD_DIM
C_SV = C_SK + SWA_KV_HEADS * SWA_HEAD_DIM
C_DT = C_SV + SWA_KV_HEADS * SWA_HEAD_DIM
D_IN_PACKED = C_DT + LANES

FF_TILE = 256
FF_CHUNKS = D_FF // FF_TILE
CONV_BUFS = 4


def _params(semantics, flags=None):
    return pltpu.CompilerParams(dimension_semantics=semantics, vmem_limit_bytes=VMEM_LIMIT, flags=flags)


def _silu(v):
    return v * jax.nn.sigmoid(v)


def _norm_mod(x, g, sc, sh):
    ms = jnp.mean(x * x, axis=-1, keepdims=True)
    y = x * lax.rsqrt(ms + EPS)
    return (y * g) * (1.0 + sc) + sh


def _rms(x, g):
    ms = jnp.mean(x * x, axis=-1, keepdims=True)
    return (x * lax.rsqrt(ms + EPS)) * g


def _mod_parts(m):
    d = D_MODEL
    return [m[:, k * d:(k + 1) * d] for k in range(6)]


def _ada_kernel(c_ref, w_ref, b_ref, o_ref):
    ca = _silu(c_ref[...]).astype(BF16)
    o_ref[0] = jnp.dot(ca, w_ref[0].astype(BF16), preferred_element_type=F32) + b_ref[0]


def _ada_call(c, ada_w, ada_b):
    nl, d, n6 = ada_w.shape
    bsz = c.shape[0]
    tn = 1536
    return pl.pallas_call(
        _ada_kernel,
        grid=(nl, n6 // tn),
        in_specs=[
            pl.BlockSpec((bsz, d), lambda l, j: (0, 0)),
            pl.BlockSpec((1, d, tn), lambda l, j: (l, 0, j)),
            pl.BlockSpec((1, 1, tn), lambda l, j: (l, 0, j)),
        ],
        out_specs=pl.BlockSpec((1, bsz, tn), lambda l, j: (l, 0, j)),
        out_shape=jax.ShapeDtypeStruct((nl, bsz, n6), F32),
        compiler_params=_params(("parallel", "parallel")),
        name="ada_mod",
    )(c, ada_w, ada_b.reshape(nl, 1, n6))


def _rope_kernel(pos_ref, f_ref, sg_ref, ct_ref, st_ref):
    ang = pos_ref[0] * f_ref[...]
    ct_ref[0] = jnp.cos(ang)
    st_ref[0] = jnp.sin(ang) * sg_ref[...]


def _rope_call(positions):
    bsz, s = positions.shape
    ts = min(s, 1024)
    half = MLA_ROPE // 2
    inv_freq = ROPE_THETA ** (-jnp.arange(0, MLA_ROPE, 2, dtype=F32) / MLA_ROPE)
    zeros = jnp.zeros((MLA_NOPE,), F32)
    pad = jnp.zeros((LANES - MLA_NOPE - MLA_ROPE,), F32)
    freq = jnp.concatenate([zeros, inv_freq, inv_freq, pad]).reshape(1, LANES)
    sign = jnp.concatenate([zeros, -jnp.ones((half,), F32), jnp.ones((half,), F32), pad]).reshape(1, LANES)
    pos = positions.astype(F32).reshape(bsz, s, 1)
    return pl.pallas_call(
        _rope_kernel,
        grid=(bsz, s // ts),
        in_specs=[
            pl.BlockSpec((1, ts, 1), lambda b, i: (b, i, 0)),
            pl.BlockSpec((1, LANES), lambda b, i: (0, 0)),
            pl.BlockSpec((1, LANES), lambda b, i: (0, 0)),
        ],
        out_specs=[pl.BlockSpec((1, ts, LANES), lambda b, i: (b, i, 0))] * 2,
        out_shape=[jax.ShapeDtypeStruct((bsz, s, LANES), F32)] * 2,
        compiler_params=_params(("parallel", "parallel")),
        name="rope_tables",
    )(pos, freq, sign)


def _apply_rope(v, ct, st):
    lane = lax.broadcasted_iota(jnp.int32, v.shape, 1)
    first_half = lane < MLA_NOPE + MLA_ROPE // 2
    partner = jnp.where(first_half,
                        pltpu.roll(v, LANES - MLA_ROPE // 2, 1),
                        pltpu.roll(v, MLA_ROPE // 2, 1))
    return v * ct + partner * st


def _prenorm_kernel(x_ref, mod_ref, g_ref, h_ref):
    sh1, sc1 = _mod_parts(mod_ref[0])[:2]
    h_ref[0] = _norm_mod(x_ref[0], g_ref[0], sc1, sh1).astype(BF16)


def _prenorm_call(x, mod3, norm_g, l, tm):
    bsz, s, d = x.shape
    return pl.pallas_call(
        _prenorm_kernel,
        grid=(bsz, s // tm),
        in_specs=[
            pl.BlockSpec((1, tm, d), lambda b, i: (b, i, 0)),
            pl.BlockSpec((1, 1, 6 * d), lambda b, i: (l * bsz + b, 0, 0)),
            pl.BlockSpec((1, 1, d), lambda b, i: (l, 0, 0)),
        ],
        out_specs=pl.BlockSpec((1, tm, d), lambda b, i: (b, i, 0)),
        out_shape=jax.ShapeDtypeStruct((bsz, s, d), BF16),
        compiler_params=_params(("parallel", "parallel")),
        name="prenorm",
    )(x, mod3, norm_g)


def _inproj_kernel(h_ref, w_ref, qg_ref, wuq_ref, kvg_ref, wuk_ref, wuv_ref, ct_ref, st_ref, dtb_ref,
                   z_ref, xbc_ref, dt_ref, qm_ref, km_ref, vm_ref, sq_ref, sk_ref, sv_ref, *, tm):
    h = h_ref[0]

    def proj(c0, width):
        return jnp.dot(h, w_ref[0, :, c0:c0 + width], preferred_element_type=F32)

    z_ref[0] = proj(C_Z, SSD_INNER)
    xbc_ref[0] = proj(C_XBC, SSD_XBC)
    sq_ref[0] = proj(C_SQ, SWA_HEADS * SWA_HEAD_DIM).astype(BF16)
    sk_ref[0] = proj(C_SK, SWA_KV_HEADS * SWA_HEAD_DIM).astype(BF16)
    sv_ref[0] = proj(C_SV, SWA_KV_HEADS * SWA_HEAD_DIM).astype(BF16)

    dt_t = proj(C_DT, LANES).T[0:SSD_HEADS, :] + dtb_ref[0]
    dt_sp = jnp.maximum(dt_t, 0.0) + jnp.log1p(jnp.exp(-jnp.abs(dt_t)))
    for c in range(tm // SSD_CHUNK):
        dt_ref[0, c] = dt_sp[:, c * SSD_CHUNK:(c + 1) * SSD_CHUNK]

    ct = ct_ref[0]
    st = st_ref[0]
    cq = _rms(proj(C_CQ, MLA_Q_RANK), qg_ref[0]).astype(BF16)
    ckv = _rms(proj(C_CKV, MLA_KV_RANK), kvg_ref[0]).astype(BF16)
    k_rope = _apply_rope(proj(C_KR, LANES), ct, st)
    for hd in range(MLA_HEADS):
        lo, hi = hd * LANES, (hd + 1) * LANES
        qh = jnp.dot(cq, wuq_ref[0, :, lo:hi], preferred_element_type=F32)
        qm_ref[0, :, lo:hi] = _apply_rope(qh, ct, st).astype(BF16)
        kh = jnp.dot(ckv, wuk_ref[0, :, lo:hi], preferred_element_type=F32)
        km_ref[0, :, lo:hi] = (kh + k_rope).astype(BF16)
    vm_ref[0] = jnp.dot(ckv, wuv_ref[0], preferred_element_type=F32).astype(BF16)


def _inproj_call(h, wp, ct, st, l, tm):
    bsz, s, d = h.shape
    nq = s // SSD_CHUNK
    full = lambda shape: pl.BlockSpec((1,) + shape, lambda b, i: (l,) + (0,) * len(shape))
    tok = lambda width: pl.BlockSpec((1, tm, width), lambda b, i: (b, i, 0))
    kern = functools.partial(_inproj_kernel, tm=tm)
    return pl.pallas_call(
        kern,
        grid=(bsz, s // tm),
        in_specs=[
            tok(d),
            full((d, D_IN_PACKED)),
            full((1, MLA_Q_RANK)),
            full((MLA_Q_RANK, MLA_HEADS * LANES)),
            full((1, MLA_KV_RANK)),
            full((MLA_KV_RANK, MLA_HEADS * LANES)),
            full((MLA_KV_RANK, MLA_HEADS * MLA_V)),
            tok(LANES),
            tok(LANES),
            full((SSD_HEADS, 1)),
        ],
        out_specs=[
            tok(SSD_INNER),
            tok(SSD_XBC),
            pl.BlockSpec((1, tm // SSD_CHUNK, SSD_HEADS, SSD_CHUNK), lambda b, i: (b, i, 0, 0)),
            tok(MLA_HEADS * LANES),
            tok(MLA_HEADS * LANES),
            tok(MLA_HEADS * MLA_V),
            tok(SWA_HEADS * SWA_HEAD_DIM),
            tok(SWA_KV_HEADS * SWA_HEAD_DIM),
            tok(SWA_KV_HEADS * SWA_HEAD_DIM),
        ],
        out_shape=[
            jax.ShapeDtypeStruct((bsz, s, SSD_INNER), F32),
            jax.ShapeDtypeStruct((bsz, s, SSD_XBC), F32),
            jax.ShapeDtypeStruct((bsz, nq, SSD_HEADS, SSD_CHUNK), F32),
            jax.ShapeDtypeStruct((bsz, s, MLA_HEADS * LANES), BF16),
            jax.ShapeDtypeStruct((bsz, s, MLA_HEADS * LANES), BF16),
            jax.ShapeDtypeStruct((bsz, s, MLA_HEADS * MLA_V), BF16),
            jax.ShapeDtypeStruct((bsz, s, SWA_HEADS * SWA_HEAD_DIM), BF16),
            jax.ShapeDtypeStruct((bsz, s, SWA_KV_HEADS * SWA_HEAD_DIM), BF16),
            jax.ShapeDtypeStruct((bsz, s, SWA_KV_HEADS * SWA_HEAD_DIM), BF16),
        ],
        compiler_params=_params(("parallel", "parallel")),
        name="in_proj",
    )(h, wp["w_in"], wp["q_norm_g"], wp["w_uq"], wp["kv_norm_g"], wp["w_uk"], wp["w_uv"], ct, st,
      wp["dt_bias"])


def _cumsum_lanes(v):
    lane = lax.broadcasted_iota(jnp.int32, v.shape, 1)
    d = 1
    while d < v.shape[1]:
        v = v + jnp.where(lane >= d, pltpu.roll(v, d, 1), 0.0)
        d *= 2
    return v


def _ssd_kernel(z_ref, xbc_ref, dt_ref, cw_ref, cb_ref, alog_ref, dsk_ref, ng_ref, y_ref,
                xpad_ref, xc_ref, h_ref, *, ts):
    q = SSD_CHUNK
    n = SSD_STATE
    halo = SUBLANES
    i = pl.program_id(1)

    @pl.when(i == 0)
    def _():
        h_ref[...] = jnp.zeros_like(h_ref)
        xpad_ref[0:halo, :] = jnp.zeros((halo, SSD_XBC), F32)

    xpad_ref[halo:halo + ts, :] = xbc_ref[0]
    cw = cw_ref[0]
    xc = cb_ref[0]
    for k in range(SSD_CONV):
        off = halo - (SSD_CONV - 1) + k
        xc = xc + xpad_ref[off:off + ts, :] * cw[k:k + 1, :]
    xc_ref[...] = _silu(xc)
    xpad_ref[0:halo, :] = xpad_ref[ts:ts + halo, :]

    a_neg = -jnp.exp(alog_ref[0])
    row = lax.broadcasted_iota(jnp.int32, (q, q), 0)
    col = lax.broadcasted_iota(jnp.int32, (q, q), 1)
    causal = col <= row
    low_half = lax.broadcasted_iota(jnp.int32, (q, LANES), 1) < HALF
    low_half_n = lax.broadcasted_iota(jnp.int32, (n, LANES), 1) < HALF
    low_half_1 = lax.broadcasted_iota(jnp.int32, (1, LANES), 1) < HALF
    heads_per_group = SSD_HEADS // SSD_GROUPS

    def chunk(c, carry):
        r0 = pl.multiple_of(c * q, q)
        dt = dt_ref[0, c]
        a_rows = _cumsum_lanes(dt * a_neg)
        ss = [None] * SSD_GROUPS
        y_pairs = [None] * (SSD_HEADS // 2)
        for g in range(SSD_GROUPS):
            b_g = xc_ref[pl.ds(r0, q), SSD_INNER + g * n:SSD_INNER + (g + 1) * n]
            c_g = xc_ref[pl.ds(r0, q), SSD_INNER + (SSD_GROUPS + g) * n:SSD_INNER + (SSD_GROUPS + g + 1) * n]
            cb = lax.dot_general(c_g.astype(BF16), b_g.astype(BF16), (((1,), (1,)), ((), ())),
                                 preferred_element_type=F32)
            b_t = b_g.T
            for jj in range(heads_per_group // 2):
                j = g * (heads_per_group // 2) + jj
                lanes = slice(j * LANES, (j + 1) * LANES)
                x_pair = xc_ref[pl.ds(r0, q), lanes]
                x_pair_bf = x_pair.astype(BF16)
                h_pair = h_ref[:, lanes]
                h_pair_bf = h_pair.astype(BF16)
                ys = []
                ds = []
                cds = []
                for hh in (2 * j, 2 * j + 1):
                    a_row = a_rows[hh:hh + 1, :]
                    dt_row = dt[hh:hh + 1, :]
                    row_bc = jnp.broadcast_to(a_row, (q, q))
                    col_bc = row_bc.T
                    decay = jnp.exp(jnp.where(causal, col_bc - row_bc, -jnp.inf))
                    m = (cb * decay) * dt_row
                    c_scaled = c_g * jnp.exp(col_bc)
                    y_h = (jnp.dot(m.astype(BF16), x_pair_bf, preferred_element_type=F32)
                           + jnp.dot(c_scaled.astype(BF16), h_pair_bf, preferred_element_type=F32))
                    a_last = a_row[:, q - 1:q]
                    w_row = jnp.exp(a_last - a_row) * dt_row
                    d_state = jnp.dot((b_t * w_row).astype(BF16), x_pair_bf, preferred_element_type=F32)
                    ys.append(y_h)
                    ds.append(d_state)
                    cds.append(jnp.exp(a_last))
                h_ref[:, lanes] = (h_pair * jnp.where(low_half_1, cds[0], cds[1])
                                   + jnp.where(low_half_n, ds[0], ds[1]))
                y_pair = jnp.where(low_half, ys[0], ys[1]) + x_pair * dsk_ref[0, :, lanes]
                y_pair = y_pair * _silu(z_ref[0, pl.ds(r0, q), lanes])
                y_pairs[j] = y_pair
                sq = jnp.sum(y_pair * y_pair, axis=-1, keepdims=True)
                ss[g] = sq if ss[g] is None else ss[g] + sq
        group_width = SSD_INNER // SSD_GROUPS
        for j in range(SSD_HEADS // 2):
            g = j // (heads_per_group // 2)
            lanes = slice(j * LANES, (j + 1) * LANES)
            inv = lax.rsqrt(ss[g] / group_width + EPS)
            y_ref[0, pl.ds(r0, q), lanes] = ((y_pairs[j] * inv) * ng_ref[0, :, lanes]).astype(BF16)
        return carry

    lax.fori_loop(0, ts // q, chunk, 0)


def _ssd_call(z, xbc, dt, wp, l, ts):
    bsz, s, _ = z.shape
    full = lambda shape: pl.BlockSpec((1,) + shape, lambda b, i: (l,) + (0,) * len(shape))
    tok = lambda width: pl.BlockSpec((1, ts, width), lambda b, i: (b, i, 0))
    kern = functools.partial(_ssd_kernel, ts=ts)
    return pl.pallas_call(
        kern,
        grid=(bsz, s // ts),
        in_specs=[
            tok(SSD_INNER),
            tok(SSD_XBC),
            pl.BlockSpec((1, ts // SSD_CHUNK, SSD_HEADS, SSD_CHUNK), lambda b, i: (b, i, 0, 0)),
            full((SSD_CONV, SSD_XBC)),
            full((1, SSD_XBC)),
            full((SSD_HEADS, 1)),
            full((1, SSD_INNER)),
            full((1, SSD_INNER)),
        ],
        out_specs=tok(SSD_INNER),
        out_shape=jax.ShapeDtypeStruct((bsz, s, SSD_INNER), BF16),
        scratch_shapes=[
            pltpu.VMEM((ts + SUBLANES, SSD_XBC), F32),
            pltpu.VMEM((ts, SSD_XBC), F32),
            pltpu.VMEM((SSD_STATE, SSD_INNER), F32),
        ],
        compiler_params=_params(("parallel", "arbitrary")),
        name="ssd_mixer",
    )(z, xbc, dt, wp["ssd_conv_w"], wp["ssd_conv_b"], wp["ssd_a_log"], wp["ssd_d_lanes"], wp["ssd_norm_g"])


def _mla_kernel(q_ref, k_ref, v_ref, o_ref, m_ref, l_ref, acc_ref, *, tq, scale):
    qi = pl.program_id(1)
    row = lax.broadcasted_iota(jnp.int32, (tq, tq), 0)
    col = lax.broadcasted_iota(jnp.int32, (tq, tq), 1)
    causal = col <= row
    low_half = lax.broadcasted_iota(jnp.int32, (tq, LANES), 1) < HALF
    reps = tq // LANES
    m_ref[...] = jnp.full(m_ref.shape, -jnp.inf, F32)
    l_ref[...] = jnp.zeros(l_ref.shape, F32)
    acc_ref[...] = jnp.zeros(acc_ref.shape, F32)

    def step(j, masked):
        r0 = pl.multiple_of(j * tq, tq)
        for hd in range(MLA_HEADS):
            k_lanes = slice(hd * LANES, (hd + 1) * LANES)
            v_lanes = slice((hd // 2) * LANES, (hd // 2 + 1) * LANES)
            q = q_ref[0, :, k_lanes]
            k = k_ref[0, pl.ds(r0, tq), k_lanes]
            v = v_ref[0, pl.ds(r0, tq), v_lanes]
            s = lax.dot_general(q, k, (((1,), (1,)), ((), ())), preferred_element_type=F32) * scale
            if masked:
                s = jnp.where(causal, s, -jnp.inf)
            m_prev = m_ref[hd]
            m_new = jnp.maximum(m_prev, jnp.max(s, axis=-1, keepdims=True))
            alpha = jnp.exp(m_prev - m_new)
            p = jnp.exp(s - jnp.concatenate([m_new] * reps, axis=1))
            l_ref[hd] = alpha * l_ref[hd] + jnp.sum(p, axis=-1, keepdims=True)
            acc_ref[hd] = alpha * acc_ref[hd] + jnp.dot(p.astype(BF16), v, preferred_element_type=F32)
            m_ref[hd] = m_new

    def body(j, carry):
        step(j, False)
        return carry

    lax.fori_loop(0, qi, body, 0)
    step(qi, True)
    for j in range(MLA_HEADS // 2):
        lo = acc_ref[2 * j] / l_ref[2 * j]
        hi = acc_ref[2 * j + 1] / l_ref[2 * j + 1]
        o_ref[0, :, j * LANES:(j + 1) * LANES] = jnp.where(low_half, lo, hi).astype(BF16)


def _mla_call(qm, km, vm, tq):
    bsz, s, _ = qm.shape
    kern = functools.partial(_mla_kernel, tq=tq, scale=1.0 / math.sqrt(MLA_NOPE + MLA_ROPE))
    return pl.pallas_call(
        kern,
        grid=(bsz, s // tq),
        in_specs=[
            pl.BlockSpec((1, tq, MLA_HEADS * LANES), lambda b, i: (b, i, 0)),
            pl.BlockSpec((1, s, MLA_HEADS * LANES), lambda b, i: (b, 0, 0)),
            pl.BlockSpec((1, s, MLA_HEADS * MLA_V), lambda b, i: (b, 0, 0)),
        ],
        out_specs=pl.BlockSpec((1, tq, MLA_HEADS * MLA_V), lambda b, i: (b, i, 0)),
        out_shape=jax.ShapeDtypeStruct((bsz, s, MLA_HEADS * MLA_V), BF16),
        scratch_shapes=[
            pltpu.VMEM((MLA_HEADS, tq, LANES), F32),
            pltpu.VMEM((MLA_HEADS, tq, LANES), F32),
            pltpu.VMEM((MLA_HEADS, tq, LANES), F32),
        ],
        compiler_params=_params(("parallel", "parallel")),
        name="mla_attention",
    )(qm, km, vm)


def _swa_kernel(sink_ref, q_ref, k_ref, v_ref, kp_ref, vp_ref, o_ref, *, ts, l):
    w = WINDOW
    i = pl.program_id(1)
    nrow = SWA_HEADS * w
    row = lax.broadcasted_iota(jnp.int32, (nrow, 2 * w), 0)
    col = lax.broadcasted_iota(jnp.int32, (nrow, 2 * w), 1)
    qpos = jnp.bitwise_and(row, w - 1)
    in_window = jnp.logical_and(col > qpos, col <= qpos + w)
    lane = lax.broadcasted_iota(jnp.int32, (w, LANES), 1)
    low_half = lane < HALF
    head_row = lax.broadcasted_iota(jnp.int32, (nrow, 1), 0)
    sink = jnp.where(head_row < w, sink_ref[l, 0],
                     jnp.where(head_row < 2 * w, sink_ref[l, 1],
                               jnp.where(head_row < 3 * w, sink_ref[l, 2], sink_ref[l, 3])))
    scale = 1.0 / math.sqrt(SWA_HEAD_DIM)
    nt = (((1,), (1,)), ((), ()))
    for blk in range(ts // w):
        r = slice(blk * w, (blk + 1) * w)
        if blk == 0:
            k_band = jnp.concatenate([kp_ref[0], k_ref[0, r, :]], axis=0)
            v_band = jnp.concatenate([vp_ref[0], v_ref[0, r, :]], axis=0)
            first_key = jnp.where(i > 0, 0, w)
            valid = jnp.logical_and(in_window, col >= first_key)
        else:
            band = slice((blk - 1) * w, (blk + 1) * w)
            k_band = k_ref[0, band, :]
            v_band = v_ref[0, band, :]
            valid = in_window
        q0 = q_ref[0, r, 0:LANES]
        q1 = q_ref[0, r, LANES:2 * LANES]
        zero = jnp.zeros_like(q0)
        q_all = jnp.concatenate([jnp.where(low_half, q0, zero), jnp.where(low_half, q1, zero),
                                 jnp.where(low_half, zero, q0), jnp.where(low_half, zero, q1)], axis=0)
        s = lax.dot_general(q_all, k_band, nt, preferred_element_type=F32) * scale
        s = jnp.where(valid, s, -jnp.inf)
        m = jnp.maximum(jnp.max(s, axis=-1, keepdims=True), sink)
        p = jnp.exp(s - m)
        denom = jnp.sum(p, axis=-1, keepdims=True) + jnp.exp(sink - m)
        pv = jnp.dot(p.astype(BF16), v_band, preferred_element_type=F32) / denom
        o_ref[0, r, 0:LANES] = jnp.where(low_half, pv[0:w], pv[2 * w:3 * w]).astype(BF16)
        o_ref[0, r, LANES:2 * LANES] = jnp.where(low_half, pv[w:2 * w], pv[3 * w:4 * w]).astype(BF16)


def _swa_call(sq, sk, sv, sinks, l, ts):
    bsz, s, _ = sq.shape
    w = WINDOW
    kvw = SWA_KV_HEADS * SWA_HEAD_DIM
    per = ts // w
    kern = functools.partial(_swa_kernel, ts=ts, l=l)
    prev_map = lambda b, i: (b, jnp.maximum(i * per - 1, 0), 0)
    return pl.pallas_call(
        kern,
        grid=(bsz, s // ts),
        in_specs=[
            pl.BlockSpec(memory_space=pltpu.SMEM),
            pl.BlockSpec((1, ts, SWA_HEADS * SWA_HEAD_DIM), lambda b, i: (b, i, 0)),
            pl.BlockSpec((1, ts, kvw), lambda b, i: (b, i, 0)),
            pl.BlockSpec((1, ts, kvw), lambda b, i: (b, i, 0)),
            pl.BlockSpec((1, w, kvw), prev_map),
            pl.BlockSpec((1, w, kvw), prev_map),
        ],
        out_specs=pl.BlockSpec((1, ts, SWA_HEADS * SWA_HEAD_DIM), lambda b, i: (b, i, 0)),
        out_shape=jax.ShapeDtypeStruct((bsz, s, SWA_HEADS * SWA_HEAD_DIM), BF16),
        compiler_params=_params(("parallel", "parallel")),
        name="swa_attention",
    )(sinks, sq, sk, sv, sk, sv)


def _outproj_kernel(ys_ref, ym_ref, yw_ref, x_ref, mod_ref, w_ref, g_ref, x1_ref, h2_ref):
    k1 = SSD_INNER
    k2 = k1 + MLA_HEADS * MLA_V
    y = (jnp.dot(ys_ref[0], w_ref[0, 0:k1, :], preferred_element_type=F32)
         + jnp.dot(ym_ref[0], w_ref[0, k1:k2, :], preferred_element_type=F32)
         + jnp.dot(yw_ref[0], w_ref[0, k2:, :], preferred_element_type=F32))
    _, _, g1, sh2, sc2, _ = _mod_parts(mod_ref[0])
    x1 = x_ref[0] + g1 * y
    x1_ref[0] = x1
    h2_ref[0] = _norm_mod(x1, g_ref[0], sc2, sh2).astype(BF16)


def _outproj_call(y_ssd, y_mla, y_swa, x, mod3, wp, l, tm):
    bsz, s, d = x.shape
    tok = lambda width: pl.BlockSpec((1, tm, width), lambda b, i: (b, i, 0))
    return pl.pallas_call(
        _outproj_kernel,
        grid=(bsz, s // tm),
        in_specs=[
            tok(SSD_INNER),
            tok(MLA_HEADS * MLA_V),
            tok(SWA_HEADS * SWA_HEAD_DIM),
            tok(d),
            pl.BlockSpec((1, 1, 6 * d), lambda b, i: (l * bsz + b, 0, 0)),
            pl.BlockSpec((1, d, d), lambda b, i: (l, 0, 0)),
            pl.BlockSpec((1, 1, d), lambda b, i: (l, 0, 0)),
        ],
        out_specs=[tok(d), tok(d)],
        out_shape=[jax.ShapeDtypeStruct((bsz, s, d), F32), jax.ShapeDtypeStruct((bsz, s, d), BF16)],
        compiler_params=_params(("parallel", "parallel")),
        name="out_proj",
    )(y_ssd, y_mla, y_swa, x, mod3, wp["w_out"], wp["norm2_g"])


def _ffn_kernel(h_ref, x_ref, mod_ref, modn_ref, wa_ref, wb_ref, cp_ref, wd_ref, gn_ref, *rest, tm, last):
    if last:
        x2_ref, ua_ref, ub_ref, ca_ref, cb_ref, acc_ref = rest
    else:
        x2_ref, hn_ref, ua_ref, ub_ref, ca_ref, cb_ref, acc_ref = rest
    halo = SUBLANES
    i = pl.program_id(1)

    @pl.when(i == 0)
    def _():
        ca_ref[...] = jnp.zeros_like(ca_ref)
        cb_ref[...] = jnp.zeros_like(cb_ref)

    h = h_ref[0]

    def up(j):
        for w_ref, buf_ref, carry_ref in ((wa_ref, ua_ref, ca_ref), (wb_ref, ub_ref, cb_ref)):
            buf = buf_ref.at[j % CONV_BUFS]
            u = jnp.dot(h, w_ref[0, j], preferred_element_type=F32)
            buf[0:halo, :] = carry_ref[j]
            buf[halo:halo + tm, :] = u
            carry_ref[j] = u[tm - halo:tm, :]

    def conv(buf_ref, j, taps):
        buf = buf_ref.at[j % CONV_BUFS]
        out = taps[FFN_CONV:FFN_CONV + 1, :]
        for k in range(FFN_CONV):
            off = halo - (FFN_CONV - 1) + k
            out = out + buf[off:off + tm, :] * taps[k:k + 1, :]
        return out

    up(0)
    for j in range(FF_CHUNKS):
        if j + 1 < FF_CHUNKS:
            up(j + 1)
        a = conv(ua_ref, j, cp_ref[0, 0, j])
        b = conv(ub_ref, j, cp_ref[0, 1, j])
        act = (_silu(a) * b).astype(BF16)
        part = jnp.dot(act, wd_ref[0, j], preferred_element_type=F32)
        if j == 0:
            acc_ref[...] = part
        else:
            acc_ref[...] += part

    g2 = _mod_parts(mod_ref[0])[5]
    x2 = x_ref[0] + g2 * acc_ref[...]
    if last:
        x2_ref[0] = _rms(x2, gn_ref[0])
    else:
        x2_ref[0] = x2
        sh1, sc1 = _mod_parts(modn_ref[0])[:2]
        hn_ref[0] = _norm_mod(x2, gn_ref[0], sc1, sh1).astype(BF16)


def _ffn_call(h2, x1, mod3, wp, next_g, l, next_l, tm, last):
    bsz, s, d = x1.shape
    tok = lambda width: pl.BlockSpec((1, tm, width), lambda b, i: (b, i, 0))
    once = pl.Buffered(1)
    kern = functools.partial(_ffn_kernel, tm=tm, last=last)
    gl = 0 if last else next_l
    outs = pl.pallas_call(
        kern,
        grid=(bsz, s // tm),
        in_specs=[
            tok(d),
            tok(d),
            pl.BlockSpec((1, 1, 6 * d), lambda b, i: (l * bsz + b, 0, 0)),
            pl.BlockSpec((1, 1, 6 * d), lambda b, i: (next_l * bsz + b, 0, 0)),
            pl.BlockSpec((1, FF_CHUNKS, d, FF_TILE), lambda b, i: (l, 0, 0, 0), pipeline_mode=once),
            pl.BlockSpec((1, FF_CHUNKS, d, FF_TILE), lambda b, i: (l, 0, 0, 0), pipeline_mode=once),
            pl.BlockSpec((1, 2, FF_CHUNKS, FFN_CONV + 1, FF_TILE), lambda b, i: (l, 0, 0, 0, 0)),
            pl.BlockSpec((1, FF_CHUNKS, FF_TILE, d), lambda b, i: (l, 0, 0, 0), pipeline_mode=once),
            pl.BlockSpec((1, 1, d), lambda b, i: (gl, 0, 0)),
        ],
        out_specs=[tok(d)] if last else [tok(d), tok(d)],
        out_shape=([jax.ShapeDtypeStruct((bsz, s, d), F32)] if last else
                   [jax.ShapeDtypeStruct((bsz, s, d), F32), jax.ShapeDtypeStruct((bsz, s, d), BF16)]),
        scratch_shapes=[
            pltpu.VMEM((CONV_BUFS, tm + SUBLANES, FF_TILE), F32),
            pltpu.VMEM((CONV_BUFS, tm + SUBLANES, FF_TILE), F32),
            pltpu.VMEM((FF_CHUNKS, SUBLANES, FF_TILE), F32),
            pltpu.VMEM((FF_CHUNKS, SUBLANES, FF_TILE), F32),
            pltpu.VMEM((tm, d), F32),
        ],
        compiler_params=_params(("parallel", "arbitrary")),
        name="ffn",
    )(h2, x1, mod3, mod3, wp["ffn_wa"], wp["ffn_wb"], wp["ffn_conv"], wp["ffn_wd"], next_g)
    return (outs[0], None) if last else outs


def _pack_weights(norm1_g, norm2_g, w_in, ssd_conv_w, ssd_conv_b, ssd_dt_bias, ssd_a_log, ssd_d,
                  ssd_norm_g, mla_q_norm_g, mla_w_uq, mla_kv_norm_g, mla_w_ukv, w_out, ffn_w_up,
                  ffn_conv_w, ffn_conv_b, ffn_w_down):
    nl = w_in.shape[0]
    row = lambda v: v.reshape(nl, 1, v.shape[-1])

    o_dt = SSD_INNER + SSD_XBC
    o_cq = SSD_IN
    o_ckv = o_cq + MLA_Q_RANK
    o_kr = o_ckv + MLA_KV_RANK
    o_sq = SSD_IN + MLA_IN
    o_sk = o_sq + SWA_HEADS * SWA_HEAD_DIM
    o_sv = o_sk + SWA_KV_HEADS * SWA_HEAD_DIM
    hd = SWA_HEAD_DIM
    idx = []
    idx += list(range(0, SSD_INNER + SSD_XBC))
    idx += list(range(o_cq, o_cq + MLA_Q_RANK))
    idx += list(range(o_ckv, o_ckv + MLA_KV_RANK))
    idx += [-1] * MLA_NOPE + list(range(o_kr, o_kr + MLA_ROPE)) + [-1] * (LANES - MLA_NOPE - MLA_ROPE)
    for head in (0, 2, 1, 3):
        idx += list(range(o_sq + head * hd, o_sq + (head + 1) * hd))
    idx += list(range(o_sk, o_sk + SWA_KV_HEADS * hd))
    idx += list(range(o_sv, o_sv + SWA_KV_HEADS * hd))
    idx += list(range(o_dt, o_dt + SSD_HEADS)) + [-1] * (LANES - SSD_HEADS)
    assert len(idx) == D_IN_PACKED
    idx = jnp.asarray(idx, jnp.int32)
    w_in_p = jnp.where(idx >= 0, jnp.take(w_in, jnp.maximum(idx, 0), axis=2), 0.0).astype(BF16)

    dq = MLA_NOPE + MLA_ROPE
    w_uq = mla_w_uq.reshape(nl, MLA_Q_RANK, MLA_HEADS, dq)
    w_uq = jnp.pad(w_uq, ((0, 0), (0, 0), (0, 0), (0, LANES - dq))).reshape(nl, MLA_Q_RANK, MLA_HEADS * LANES)
    w_ukv = mla_w_ukv.reshape(nl, MLA_KV_RANK, MLA_HEADS, MLA_NOPE + MLA_V)
    w_uk = jnp.pad(w_ukv[..., :MLA_NOPE], ((0, 0), (0, 0), (0, 0), (0, LANES - MLA_NOPE)))
    w_uk = w_uk.reshape(nl, MLA_KV_RANK, MLA_HEADS * LANES)
    w_uv = w_ukv[..., MLA_NOPE:].reshape(nl, MLA_KV_RANK, MLA_HEADS * MLA_V)

    o_w = SSD_INNER + MLA_HEADS * MLA_V
    w_swa = w_out[:, o_w:, :].reshape(nl, SWA_HEADS, hd, D_MODEL)[:, jnp.asarray([0, 2, 1, 3])]
    w_out_p = jnp.concatenate([w_out[:, :o_w, :], w_swa.reshape(nl, SWA_HEADS * hd, D_MODEL)], axis=1)

    wa = ffn_w_up[:, :, :D_FF].reshape(nl, D_MODEL, FF_CHUNKS, FF_TILE).transpose(0, 2, 1, 3)
    wb = ffn_w_up[:, :, D_FF:].reshape(nl, D_MODEL, FF_CHUNKS, FF_TILE).transpose(0, 2, 1, 3)
    taps = jnp.concatenate([ffn_conv_w, ffn_conv_b[:, None, :]], axis=1)
    taps = taps.reshape(nl, FFN_CONV + 1, 2, FF_CHUNKS, FF_TILE).transpose(0, 2, 3, 1, 4)
    wd = ffn_w_down.reshape(nl, FF_CHUNKS, FF_TILE, D_MODEL)

    return {
        "norm1_g": row(norm1_g), "norm2_g": row(norm2_g),
        "w_in": w_in_p,
        "ssd_conv_w": ssd_conv_w, "ssd_conv_b": row(ssd_conv_b),
        "dt_bias": ssd_dt_bias.reshape(nl, SSD_HEADS, 1),
        "ssd_a_log": ssd_a_log.reshape(nl, SSD_HEADS, 1),
        "ssd_d_lanes": row(jnp.repeat(ssd_d, SSD_HEAD_DIM, axis=-1)),
        "ssd_norm_g": row(ssd_norm_g),
        "q_norm_g": row(mla_q_norm_g), "w_uq": w_uq.astype(BF16),
        "kv_norm_g": row(mla_kv_norm_g), "w_uk": w_uk.astype(BF16), "w_uv": w_uv.astype(BF16),
        "w_out": w_out_p.astype(BF16),
        "ffn_wa": wa.astype(BF16), "ffn_wb": wb.astype(BF16), "ffn_conv": taps, "ffn_wd": wd.astype(BF16),
    }


def _tile(s, want):
    t = min(s, want)
    assert s % t == 0
    return t


def kernel(x, c, positions, ada_w, ada_b, norm1_g, norm2_g, w_in, ssd_conv_w, ssd_conv_b, ssd_dt_bias,
           ssd_a_log, ssd_d, ssd_norm_g, mla_q_norm_g, mla_w_uq, mla_kv_norm_g, mla_w_ukv, swa_sinks,
           w_out, ffn_w_up, ffn_conv_w, ffn_conv_b, ffn_w_down, final_norm_g):
    bsz, s, d = x.shape
    nl = w_in.shape[0]
    assert d == D_MODEL and s % SSD_CHUNK == 0
    tm = _tile(s, 512)
    tf = _tile(s, 512)
    ts = _tile(s, 512)
    tq = _tile(s, 512)

    wp = _pack_weights(norm1_g, norm2_g, w_in, ssd_conv_w, ssd_conv_b, ssd_dt_bias, ssd_a_log, ssd_d,
                       ssd_norm_g, mla_q_norm_g, mla_w_uq, mla_kv_norm_g, mla_w_ukv, w_out, ffn_w_up,
                       ffn_conv_w, ffn_conv_b, ffn_w_down)
    final_g = final_norm_g.reshape(1, 1, d)
    mod3 = _ada_call(c, ada_w, ada_b).reshape(nl * bsz, 1, 6 * d)
    ct, st = _rope_call(positions)

    h = _prenorm_call(x, mod3, wp["norm1_g"], 0, tm)
    for l in range(nl):
        z, xbc, dt, qm, km, vm, sq, sk, sv = _inproj_call(h, wp, ct, st, l, tm)
        y_ssd = _ssd_call(z, xbc, dt, wp, l, ts)
        y_mla = _mla_call(qm, km, vm, tq)
        y_swa = _swa_call(sq, sk, sv, swa_sinks, l, ts)
        x1, h2 = _outproj_call(y_ssd, y_mla, y_swa, x, mod3, wp, l, tm)
        last = l == nl - 1
        next_l = l if last else l + 1
        next_g = final_g if last else wp["norm1_g"]
        x, h = _ffn_call(h2, x1, mod3, wp, next_g, l, next_l, tf, last)
    return x
```

```python
import functools
import math

import jax
import jax.numpy as jnp
from jax import lax
from jax.experimental import pallas as pl
from jax.experimental.pallas import tpu as pltpu

F32 = jnp.float32
BF16 = jnp.bfloat16

D_MODEL = 1024
EPS = 1e-6
SSD_HEADS = 8
SSD_HEAD_DIM = 64
SSD_INNER = SSD_HEADS * SSD_HEAD_DIM
SSD_GROUPS = 2
SSD_STATE = 128
SSD_CONV = 4
SSD_CHUNK = 128
SSD_XBC = SSD_INNER + 2 * SSD_GROUPS * SSD_STATE
MLA_HEADS = 4
MLA_NOPE = 64
MLA_ROPE = 32
MLA_V = 64
MLA_Q_RANK = 256
MLA_KV_RANK = 128
ROPE_THETA = 10000.0
SWA_HEADS = 4
SWA_KV_HEADS = 2
SWA_HEAD_DIM = 64
WINDOW = 128
D_FF = 2816
FFN_CONV = 3
SSD_IN = SSD_INNER + SSD_XBC + SSD_HEADS
MLA_IN = MLA_Q_RANK + MLA_KV_RANK + MLA_ROPE
SWA_IN = (SWA_HEADS + 2 * SWA_KV_HEADS) * SWA_HEAD_DIM
D_IN = SSD_IN + MLA_IN + SWA_IN

LANES = 128
SUBLANES = 8
HALF = LANES // 2
VMEM_LIMIT = 56 * 1024 * 1024

C_Z = 0
C_XBC = C_Z + SSD_INNER
C_CQ = C_XBC + SSD_XBC
C_CKV = C_CQ + MLA_Q_RANK
C_KR = C_CKV + MLA_KV_RANK
C_SQ = C_KR + LANES
C_SK = C_SQ + SWA_HEADS * SWA_HEAD_DIM
C_SV = C_SK + SWA_KV_HEADS * SWA_HEAD_DIM
C_DT = C_SV + SWA_KV_HEADS * SWA_HEAD_DIM
D_IN_PACKED = C_DT + LANES

FF_TILE = 256
FF_CHUNKS = D_FF // FF_TILE
CONV_BUFS = 4


def _params(semantics, flags=None):
    return pltpu.CompilerParams(dimension_semantics=semantics, vmem_limit_bytes=VMEM_LIMIT, flags=flags)


def _silu(v):
    return v * jax.nn.sigmoid(v)


def _norm_mod(x, g, sc, sh):
    ms = jnp.mean(x * x, axis=-1, keepdims=True)
    y = x * lax.rsqrt(ms + EPS)
    return (y * g) * (1.0 + sc) + sh


def _rms(x, g):
    ms = jnp.mean(x * x, axis=-1, keepdims=True)
    return (x * lax.rsqrt(ms + EPS)) * g


def _mod_parts(m):
    d = D_MODEL
    return [m[:, k * d:(k + 1) * d] for k in range(6)]


def _ada_kernel(c_ref, w_ref, b_ref, o_ref):
    ca = _silu(c_ref[...]).astype(BF16)
    o_ref[0] = jnp.dot(ca, w_ref[0].astype(BF16), preferred_element_type=F32) + b_ref[0]


def _ada_call(c, ada_w, ada_b):
    nl, d, n6 = ada_w.shape
    bsz = c.shape[0]
    tn = 1536
    return pl.pallas_call(
        _ada_kernel,
        grid=(nl, n6 // tn),
        in_specs=[
            pl.BlockSpec((bsz, d), lambda l, j: (0, 0)),
            pl.BlockSpec((1, d, tn), lambda l, j: (l, 0, j)),
            pl.BlockSpec((1, 1, tn), lambda l, j: (l, 0, j)),
        ],
        out_specs=pl.BlockSpec((1, bsz, tn), lambda l, j: (l, 0, j)),
        out_shape=jax.ShapeDtypeStruct((nl, bsz, n6), F32),
        compiler_params=_params(("parallel", "parallel")),
        name="ada_mod",
    )(c, ada_w, ada_b.reshape(nl, 1, n6))


def _rope_kernel(pos_ref, f_ref, sg_ref, ct_ref, st_ref):
    ang = pos_ref[0] * f_ref[...]
    ct_ref[0] = jnp.cos(ang)
    st_ref[0] = jnp.sin(ang) * sg_ref[...]


def _rope_call(positions):
    bsz, s = positions.shape
    ts = min(s, 1024)
    half = MLA_ROPE // 2
    inv_freq = ROPE_THETA ** (-jnp.arange(0, MLA_ROPE, 2, dtype=F32) / MLA_ROPE)
    zeros = jnp.zeros((MLA_NOPE,), F32)
    pad = jnp.zeros((LANES - MLA_NOPE - MLA_ROPE,), F32)
    freq = jnp.concatenate([zeros, inv_freq, inv_freq, pad]).reshape(1, LANES)
    sign = jnp.concatenate([zeros, -jnp.ones((half,), F32), jnp.ones((half,), F32), pad]).reshape(1, LANES)
    pos = positions.astype(F32).reshape(bsz, s, 1)
    return pl.pallas_call(
        _rope_kernel,
        grid=(bsz, s // ts),
        in_specs=[
            pl.BlockSpec((1, ts, 1), lambda b, i: (b, i, 0)),
            pl.BlockSpec((1, LANES), lambda b, i: (0, 0)),
            pl.BlockSpec((1, LANES), lambda b, i: (0, 0)),
        ],
        out_specs=[pl.BlockSpec((1, ts, LANES), lambda b, i: (b, i, 0))] * 2,
        out_shape=[jax.ShapeDtypeStruct((bsz, s, LANES), F32)] * 2,
        compiler_params=_params(("parallel", "parallel")),
        name="rope_tables",
    )(pos, freq, sign)


def _apply_rope(v, ct, st):
    lane = lax.broadcasted_iota(jnp.int32, v.shape, 1)
    first_half = lane < MLA_NOPE + MLA_ROPE // 2
    partner = jnp.where(first_half,
                        pltpu.roll(v, LANES - MLA_ROPE // 2, 1),
                        pltpu.roll(v, MLA_ROPE // 2, 1))
    return v * ct + partner * st


def _prenorm_kernel(x_ref, mod_ref, g_ref, h_ref):
    sh1, sc1 = _mod_parts(mod_ref[0])[:2]
    h_ref[0] = _norm_mod(x_ref[0], g_ref[0], sc1, sh1).astype(BF16)


def _prenorm_call(x, mod3, norm_g, l, tm):
    bsz, s, d = x.shape
    return pl.pallas_call(
        _prenorm_kernel,
        grid=(bsz, s // tm),
        in_specs=[
            pl.BlockSpec((1, tm, d), lambda b, i: (b, i, 0)),
            pl.BlockSpec((1, 1, 6 * d), lambda b, i: (l * bsz + b, 0, 0)),
            pl.BlockSpec((1, 1, d), lambda b, i: (l, 0, 0)),
        ],
        out_specs=pl.BlockSpec((1, tm, d), lambda b, i: (b, i, 0)),
        out_shape=jax.ShapeDtypeStruct((bsz, s, d), BF16),
        compiler_params=_params(("parallel", "parallel")),
        name="prenorm",
    )(x, mod3, norm_g)


def _inproj_kernel(h_ref, w_ref, qg_ref, wuq_ref, kvg_ref, wuk_ref, wuv_ref, ct_ref, st_ref, dtb_ref,
                   z_ref, xbc_ref, dt_ref, qm_ref, km_ref, vm_ref, sq_ref, sk_ref, sv_ref, *, tm):
    h = h_ref[0]

    def proj(c0, width):
        return jnp.dot(h, w_ref[0, :, c0:c0 + width], preferred_element_type=F32)

    rest = proj(C_CQ, D_IN_PACKED - C_CQ)
    part = lambda c0, width: rest[:, c0 - C_CQ:c0 - C_CQ + width]
    sq_ref[0] = part(C_SQ, SWA_HEADS * SWA_HEAD_DIM).astype(BF16)
    sk_ref[0] = part(C_SK, SWA_KV_HEADS * SWA_HEAD_DIM).astype(BF16)
    sv_ref[0] = part(C_SV, SWA_KV_HEADS * SWA_HEAD_DIM).astype(BF16)

    dt_t = part(C_DT, LANES).T[0:SSD_HEADS, :] + dtb_ref[0]
    dt_sp = jnp.maximum(dt_t, 0.0) + jnp.log1p(jnp.exp(-jnp.abs(dt_t)))
    for c in range(tm // SSD_CHUNK):
        dt_ref[0, c] = dt_sp[:, c * SSD_CHUNK:(c + 1) * SSD_CHUNK]

    ct = ct_ref[0]
    st = st_ref[0]
    cq = _rms(part(C_CQ, MLA_Q_RANK), qg_ref[0]).astype(BF16)
    ckv = _rms(part(C_CKV, MLA_KV_RANK), kvg_ref[0]).astype(BF16)
    k_rope = _apply_rope(part(C_KR, LANES), ct, st)
    q_all = jnp.dot(cq, wuq_ref[0], preferred_element_type=F32)
    k_all = jnp.dot(ckv, wuk_ref[0], preferred_element_type=F32)
    for hd in range(MLA_HEADS):
        lo, hi = hd * LANES, (hd + 1) * LANES
        qm_ref[0, :, lo:hi] = _apply_rope(q_all[:, lo:hi], ct, st).astype(BF16)
        km_ref[0, :, lo:hi] = (k_all[:, lo:hi] + k_rope).astype(BF16)
    vm_ref[0] = jnp.dot(ckv, wuv_ref[0], preferred_element_type=F32).astype(BF16)
    ssd = proj(C_Z, C_CQ - C_Z)
    z_ref[0] = ssd[:, C_Z:C_XBC]
    xbc_ref[0] = ssd[:, C_XBC:C_CQ]


def _inproj_call(h, wp, ct, st, l, tm):
    bsz, s, d = h.shape
    nq = s // SSD_CHUNK
    full = lambda shape: pl.BlockSpec((1,) + shape, lambda b, i: (l,) + (0,) * len(shape))
    tok = lambda width: pl.BlockSpec((1, tm, width), lambda b, i: (b, i, 0))
    kern = functools.partial(_inproj_kernel, tm=tm)
    return pl.pallas_call(
        kern,
        grid=(bsz, s // tm),
        in_specs=[
            tok(d),
            full((d, D_IN_PACKED)),
            full((1, MLA_Q_RANK)),
            full((MLA_Q_RANK, MLA_HEADS * LANES)),
            full((1, MLA_KV_RANK)),
            full((MLA_KV_RANK, MLA_HEADS * LANES)),
            full((MLA_KV_RANK, MLA_HEADS * MLA_V)),
            tok(LANES),
            tok(LANES),
            full((SSD_HEADS, 1)),
        ],
        out_specs=[
            tok(SSD_INNER),
            tok(SSD_XBC),
            pl.BlockSpec((1, tm // SSD_CHUNK, SSD_HEADS, SSD_CHUNK), lambda b, i: (b, i, 0, 0)),
            tok(MLA_HEADS * LANES),
            tok(MLA_HEADS * LANES),
            tok(MLA_HEADS * MLA_V),
            tok(SWA_HEADS * SWA_HEAD_DIM),
            tok(SWA_KV_HEADS * SWA_HEAD_DIM),
            tok(SWA_KV_HEADS * SWA_HEAD_DIM),
        ],
        out_shape=[
            jax.ShapeDtypeStruct((bsz, s, SSD_INNER), F32),
            jax.ShapeDtypeStruct((bsz, s, SSD_XBC), F32),
            jax.ShapeDtypeStruct((bsz, nq, SSD_HEADS, SSD_CHUNK), F32),
            jax.ShapeDtypeStruct((bsz, s, MLA_HEADS * LANES), BF16),
            jax.ShapeDtypeStruct((bsz, s, MLA_HEADS * LANES), BF16),
            jax.ShapeDtypeStruct((bsz, s, MLA_HEADS * MLA_V), BF16),
            jax.ShapeDtypeStruct((bsz, s, SWA_HEADS * SWA_HEAD_DIM), BF16),
            jax.ShapeDtypeStruct((bsz, s, SWA_KV_HEADS * SWA_HEAD_DIM), BF16),
            jax.ShapeDtypeStruct((bsz, s, SWA_KV_HEADS * SWA_HEAD_DIM), BF16),
        ],
        compiler_params=_params(("parallel", "parallel")),
        name="in_proj",
    )(h, wp["w_in"], wp["q_norm_g"], wp["w_uq"], wp["kv_norm_g"], wp["w_uk"], wp["w_uv"], ct, st,
      wp["dt_bias"])


def _cumsum_lanes(v):
    lane = lax.broadcasted_iota(jnp.int32, v.shape, 1)
    d = 1
    while d < v.shape[1]:
        v = v + jnp.where(lane >= d, pltpu.roll(v, d, 1), 0.0)
        d *= 2
    return v


def _ssd_kernel(z_ref, xbc_ref, dt_ref, cw_ref, cb_ref, alog_ref, dsk_ref, ng_ref, y_ref,
                xpad_ref, xc_ref, h_ref, *, ts):
    q = SSD_CHUNK
    n = SSD_STATE
    halo = SUBLANES
    i = pl.program_id(1)

    @pl.when(i == 0)
    def _():
        h_ref[...] = jnp.zeros_like(h_ref)
        xpad_ref[0:halo, :] = jnp.zeros((halo, SSD_XBC), F32)

    xpad_ref[halo:halo + ts, :] = xbc_ref[0]
    cw = cw_ref[0]
    xc = cb_ref[0]
    for k in range(SSD_CONV):
        off = halo - (SSD_CONV - 1) + k
        xc = xc + xpad_ref[off:off + ts, :] * cw[k:k + 1, :]
    xc_ref[...] = _silu(xc)
    xpad_ref[0:halo, :] = xpad_ref[ts:ts + halo, :]

    a_neg = -jnp.exp(alog_ref[0])
    row = lax.broadcasted_iota(jnp.int32, (q, q), 0)
    col = lax.broadcasted_iota(jnp.int32, (q, q), 1)
    causal = col <= row
    low_half = lax.broadcasted_iota(jnp.int32, (q, LANES), 1) < HALF
    low_half_n = lax.broadcasted_iota(jnp.int32, (n, LANES), 1) < HALF
    low_half_1 = lax.broadcasted_iota(jnp.int32, (1, LANES), 1) < HALF
    heads_per_group = SSD_HEADS // SSD_GROUPS

    nch = ts // q
    dt_all = jnp.concatenate([dt_ref[0, c] for c in range(nch)], axis=0)
    a_all = _cumsum_lanes(dt_all * jnp.concatenate([a_neg] * nch, axis=0))
    a_last_all = a_all[:, q - 1:q]
    w_all = jnp.exp(a_last_all - a_all) * dt_all
    cd_all = jnp.exp(a_last_all)

    for c in range(nch):
        r0 = c * q
        dt = dt_all[c * SSD_HEADS:(c + 1) * SSD_HEADS]
        a_rows = a_all[c * SSD_HEADS:(c + 1) * SSD_HEADS]
        w_rows = w_all[c * SSD_HEADS:(c + 1) * SSD_HEADS]
        cd_rows = cd_all[c * SSD_HEADS:(c + 1) * SSD_HEADS]
        ss = [None] * SSD_GROUPS
        y_pairs = [None] * (SSD_HEADS // 2)
        for g in range(SSD_GROUPS):
            b_g = xc_ref[pl.ds(r0, q), SSD_INNER + g * n:SSD_INNER + (g + 1) * n]
            c_g = xc_ref[pl.ds(r0, q), SSD_INNER + (SSD_GROUPS + g) * n:SSD_INNER + (SSD_GROUPS + g + 1) * n]
            cb = lax.dot_general(c_g.astype(BF16), b_g.astype(BF16), (((1,), (1,)), ((), ())),
                                 preferred_element_type=F32)
            b_t = b_g.T
            for jj in range(heads_per_group // 2):
                j = g * (heads_per_group // 2) + jj
                lanes = slice(j * LANES, (j + 1) * LANES)
                x_pair = xc_ref[pl.ds(r0, q), lanes]
                x_pair_bf = x_pair.astype(BF16)
                h_pair = h_ref[:, lanes]
                h_pair_bf = h_pair.astype(BF16)
                ys = []
                ds = []
                cds = []
                for hh in (2 * j, 2 * j + 1):
                    a_row = a_rows[hh:hh + 1, :]
                    dt_row = dt[hh:hh + 1, :]
                    row_bc = jnp.broadcast_to(a_row, (q, q))
                    col_bc = row_bc.T
                    decay = jnp.exp(jnp.where(causal, col_bc - row_bc, -jnp.inf))
                    m = (cb * decay) * dt_row
                    c_scaled = c_g * jnp.exp(col_bc)
                    y_h = (jnp.dot(m.astype(BF16), x_pair_bf, preferred_element_type=F32)
                           + jnp.dot(c_scaled.astype(BF16), h_pair_bf, preferred_element_type=F32))
                    w_row = w_rows[hh:hh + 1, :]
                    d_state = jnp.dot((b_t * w_row).astype(BF16), x_pair_bf, preferred_element_type=F32)
                    ys.append(y_h)
                    ds.append(d_state)
                    cds.append(cd_rows[hh:hh + 1, :])
                h_ref[:, lanes] = (h_pair * jnp.where(low_half_1, cds[0], cds[1])
                                   + jnp.where(low_half_n, ds[0], ds[1]))
                y_pair = jnp.where(low_half, ys[0], ys[1]) + x_pair * dsk_ref[0, :, lanes]
                y_pair = y_pair * _silu(z_ref[0, pl.ds(r0, q), lanes])
                y_pairs[j] = y_pair
                sq = jnp.sum(y_pair * y_pair, axis=-1, keepdims=True)
                ss[g] = sq if ss[g] is None else ss[g] + sq
        group_width = SSD_INNER // SSD_GROUPS
        for j in range(SSD_HEADS // 2):
            g = j // (heads_per_group // 2)
            lanes = slice(j * LANES, (j + 1) * LANES)
            inv = lax.rsqrt(ss[g] / group_width + EPS)
            y_ref[0, pl.ds(r0, q), lanes] = ((y_pairs[j] * inv) * ng_ref[0, :, lanes]).astype(BF16)


def _ssd_call(z, xbc, dt, wp, l, ts):
    bsz, s, _ = z.shape
    full = lambda shape: pl.BlockSpec((1,) + shape, lambda b, i: (l,) + (0,) * len(shape))
    tok = lambda width: pl.BlockSpec((1, ts, width), lambda b, i: (b, i, 0))
    kern = functools.partial(_ssd_kernel, ts=ts)
    return pl.pallas_call(
        kern,
        grid=(bsz, s // ts),
        in_specs=[
            tok(SSD_INNER),
            tok(SSD_XBC),
            pl.BlockSpec((1, ts // SSD_CHUNK, SSD_HEADS, SSD_CHUNK), lambda b, i: (b, i, 0, 0)),
            full((SSD_CONV, SSD_XBC)),
            full((1, SSD_XBC)),
            full((SSD_HEADS, 1)),
            full((1, SSD_INNER)),
            full((1, SSD_INNER)),
        ],
        out_specs=tok(SSD_INNER),
        out_shape=jax.ShapeDtypeStruct((bsz, s, SSD_INNER), BF16),
        scratch_shapes=[
            pltpu.VMEM((ts + SUBLANES, SSD_XBC), F32),
            pltpu.VMEM((ts, SSD_XBC), F32),
            pltpu.VMEM((SSD_STATE, SSD_INNER), F32),
        ],
        compiler_params=_params(("parallel", "arbitrary")),
        name="ssd_mixer",
    )(z, xbc, dt, wp["ssd_conv_w"], wp["ssd_conv_b"], wp["ssd_a_log"], wp["ssd_d_lanes"], wp["ssd_norm_g"])


def _mla_kernel(q_ref, k_ref, v_ref, o_ref, m_ref, l_ref, acc_ref, *, tq, scale):
    qi = pl.program_id(1)
    row = lax.broadcasted_iota(jnp.int32, (tq, tq), 0)
    col = lax.broadcasted_iota(jnp.int32, (tq, tq), 1)
    causal = col <= row
    low_half = lax.broadcasted_iota(jnp.int32, (tq, LANES), 1) < HALF
    reps = tq // LANES
    scale_log2e = scale * math.log2(math.e)
    m_ref[...] = jnp.full(m_ref.shape, -jnp.inf, F32)
    l_ref[...] = jnp.zeros(l_ref.shape, F32)
    acc_ref[...] = jnp.zeros(acc_ref.shape, F32)

    def step(j, masked):
        r0 = pl.multiple_of(j * tq, tq)
        for hd in range(MLA_HEADS):
            k_lanes = slice(hd * LANES, (hd + 1) * LANES)
            v_lanes = slice((hd // 2) * LANES, (hd // 2 + 1) * LANES)
            q = q_ref[0, :, k_lanes]
            k = k_ref[0, pl.ds(r0, tq), k_lanes]
            v = v_ref[0, pl.ds(r0, tq), v_lanes]
            s = lax.dot_general(q, k, (((1,), (1,)), ((), ())), preferred_element_type=F32)
            if masked:
                s = jnp.where(causal, s, -jnp.inf)
            m_prev = m_ref[hd]
            m_new = jnp.maximum(m_prev, jnp.max(s, axis=-1, keepdims=True))
            alpha = jnp.exp2((m_prev - m_new) * scale_log2e)
            p = jnp.exp2((s - jnp.concatenate([m_new] * reps, axis=1)) * scale_log2e)
            l_ref[hd] = alpha * l_ref[hd] + jnp.sum(p, axis=-1, keepdims=True)
            acc_ref[hd] = alpha * acc_ref[hd] + jnp.dot(p.astype(BF16), v, preferred_element_type=F32)
            m_ref[hd] = m_new

    def body(j, carry):
        step(j, False)
        return carry

    lax.fori_loop(0, qi, body, 0)
    step(qi, True)
    for j in range(MLA_HEADS // 2):
        lo = acc_ref[2 * j] / l_ref[2 * j]
        hi = acc_ref[2 * j + 1] / l_ref[2 * j + 1]
        o_ref[0, :, j * LANES:(j + 1) * LANES] = jnp.where(low_half, lo, hi).astype(BF16)


def _mla_call(qm, km, vm, tq):
    bsz, s, _ = qm.shape
    kern = functools.partial(_mla_kernel, tq=tq, scale=1.0 / math.sqrt(MLA_NOPE + MLA_ROPE))
    return pl.pallas_call(
        kern,
        grid=(bsz, s // tq),
        in_specs=[
            pl.BlockSpec((1, tq, MLA_HEADS * LANES), lambda b, i: (b, i, 0)),
            pl.BlockSpec((1, s, MLA_HEADS * LANES), lambda b, i: (b, 0, 0)),
            pl.BlockSpec((1, s, MLA_HEADS * MLA_V), lambda b, i: (b, 0, 0)),
        ],
        out_specs=pl.BlockSpec((1, tq, MLA_HEADS * MLA_V), lambda b, i: (b, i, 0)),
        out_shape=jax.ShapeDtypeStruct((bsz, s, MLA_HEADS * MLA_V), BF16),
        scratch_shapes=[
            pltpu.VMEM((MLA_HEADS, tq, LANES), F32),
            pltpu.VMEM((MLA_HEADS, tq, LANES), F32),
            pltpu.VMEM((MLA_HEADS, tq, LANES), F32),
        ],
        compiler_params=_params(("parallel", "parallel")),
        name="mla_attention",
    )(qm, km, vm)


def _swa_kernel(sink_ref, q_ref, k_ref, v_ref, kp_ref, vp_ref, o_ref, *, ts, l):
    w = WINDOW
    i = pl.program_id(1)
    nrow = SWA_HEADS * w
    row = lax.broadcasted_iota(jnp.int32, (nrow, 2 * w), 0)
    col = lax.broadcasted_iota(jnp.int32, (nrow, 2 * w), 1)
    qpos = jnp.bitwise_and(row, w - 1)
    in_window = jnp.logical_and(col > qpos, col <= qpos + w)
    lane = lax.broadcasted_iota(jnp.int32, (w, LANES), 1)
    low_half = lane < HALF
    head_row = lax.broadcasted_iota(jnp.int32, (nrow, 1), 0)
    sink = jnp.where(head_row < w, sink_ref[l, 0],
                     jnp.where(head_row < 2 * w, sink_ref[l, 1],
                               jnp.where(head_row < 3 * w, sink_ref[l, 2], sink_ref[l, 3])))
    scale = 1.0 / math.sqrt(SWA_HEAD_DIM)
    nt = (((1,), (1,)), ((), ()))
    for blk in range(ts // w):
        r = slice(blk * w, (blk + 1) * w)
        if blk == 0:
            k_band = jnp.concatenate([kp_ref[0], k_ref[0, r, :]], axis=0)
            v_band = jnp.concatenate([vp_ref[0], v_ref[0, r, :]], axis=0)
            first_key = jnp.where(i > 0, 0, w)
            valid = jnp.logical_and(in_window, col >= first_key)
        else:
            band = slice((blk - 1) * w, (blk + 1) * w)
            k_band = k_ref[0, band, :]
            v_band = v_ref[0, band, :]
            valid = in_window
        q0 = q_ref[0, r, 0:LANES]
        q1 = q_ref[0, r, LANES:2 * LANES]
        zero = jnp.zeros_like(q0)
        q_all = jnp.concatenate([jnp.where(low_half, q0, zero), jnp.where(low_half, q1, zero),
                                 jnp.where(low_half, zero, q0), jnp.where(low_half, zero, q1)], axis=0)
        s = lax.dot_general(q_all, k_band, nt, preferred_element_type=F32) * scale
        s = jnp.where(valid, s, -jnp.inf)
        m = jnp.maximum(jnp.max(s, axis=-1, keepdims=True), sink)
        p = jnp.exp(s - m)
        denom = jnp.sum(p, axis=-1, keepdims=True) + jnp.exp(sink - m)
        pv = jnp.dot(p.astype(BF16), v_band, preferred_element_type=F32) / denom
        o_ref[0, r, 0:LANES] = jnp.where(low_half, pv[0:w], pv[2 * w:3 * w]).astype(BF16)
        o_ref[0, r, LANES:2 * LANES] = jnp.where(low_half, pv[w:2 * w], pv[3 * w:4 * w]).astype(BF16)


def _swa_call(sq, sk, sv, sinks, l, ts):
    bsz, s, _ = sq.shape
    w = WINDOW
    kvw = SWA_KV_HEADS * SWA_HEAD_DIM
    per = ts // w
    kern = functools.partial(_swa_kernel, ts=ts, l=l)
    prev_map = lambda b, i: (b, jnp.maximum(i * per - 1, 0), 0)
    return pl.pallas_call(
        kern,
        grid=(bsz, s // ts),
        in_specs=[
            pl.BlockSpec(memory_space=pltpu.SMEM),
            pl.BlockSpec((1, ts, SWA_HEADS * SWA_HEAD_DIM), lambda b, i: (b, i, 0)),
            pl.BlockSpec((1, ts, kvw), lambda b, i: (b, i, 0)),
            pl.BlockSpec((1, ts, kvw), lambda b, i: (b, i, 0)),
            pl.BlockSpec((1, w, kvw), prev_map),
            pl.BlockSpec((1, w, kvw), prev_map),
        ],
        out_specs=pl.BlockSpec((1, ts, SWA_HEADS * SWA_HEAD_DIM), lambda b, i: (b, i, 0)),
        out_shape=jax.ShapeDtypeStruct((bsz, s, SWA_HEADS * SWA_HEAD_DIM), BF16),
        compiler_params=_params(("parallel", "parallel")),
        name="swa_attention",
    )(sinks, sq, sk, sv, sk, sv)


def _mlp_kernel(ys_ref, ym_ref, yw_ref, x_ref, mod_ref, modn_ref, wo_ref, g2_ref, wa_ref, wb_ref, cp_ref,
                wd_ref, gn_ref, *rest, tm, last):
    if last:
        x2_ref, h_ref, x1_ref, ua_ref, ub_ref, ca_ref, cb_ref, acc_ref = rest
    else:
        x2_ref, hn_ref, h_ref, x1_ref, ua_ref, ub_ref, ca_ref, cb_ref, acc_ref = rest
    halo = SUBLANES
    i = pl.program_id(1)

    @pl.when(i == 0)
    def _():
        ca_ref[...] = jnp.zeros_like(ca_ref)
        cb_ref[...] = jnp.zeros_like(cb_ref)

    k1 = SSD_INNER
    k2 = k1 + MLA_HEADS * MLA_V
    y = (jnp.dot(ys_ref[0], wo_ref[0, 0:k1, :], preferred_element_type=F32)
         + jnp.dot(ym_ref[0], wo_ref[0, k1:k2, :], preferred_element_type=F32)
         + jnp.dot(yw_ref[0], wo_ref[0, k2:, :], preferred_element_type=F32))
    _, _, g1, sh2, sc2, g2 = _mod_parts(mod_ref[0])
    x1 = x_ref[0] + g1 * y
    x1_ref[...] = x1
    h_ref[...] = _norm_mod(x1, g2_ref[0], sc2, sh2).astype(BF16)

    def up(j):
        for w_ref, buf_ref, carry_ref in ((wa_ref, ua_ref, ca_ref), (wb_ref, ub_ref, cb_ref)):
            buf = buf_ref.at[j % CONV_BUFS]
            u = jnp.dot(h_ref[...], w_ref[0, j], preferred_element_type=F32)
            buf[0:halo, :] = carry_ref[j]
            buf[halo:halo + tm, :] = u
            carry_ref[j] = u[tm - halo:tm, :]

    def conv(buf_ref, j, taps):
        buf = buf_ref.at[j % CONV_BUFS]
        out = taps[FFN_CONV:FFN_CONV + 1, :]
        for k in range(FFN_CONV):
            off = halo - (FFN_CONV - 1) + k
            out = out + buf[off:off + tm, :] * taps[k:k + 1, :]
        return out

    up(0)
    for j in range(FF_CHUNKS):
        if j + 1 < FF_CHUNKS:
            up(j + 1)
        a = conv(ua_ref, j, cp_ref[0, 0, j])
        b = conv(ub_ref, j, cp_ref[0, 1, j])
        act = (_silu(a) * b).astype(BF16)
        part = jnp.dot(act, wd_ref[0, j], preferred_element_type=F32)
        if j == 0:
            acc_ref[...] = part
        else:
            acc_ref[...] += part

    x2 = x1_ref[...] + g2 * acc_ref[...]
    if last:
        x2_ref[0] = _rms(x2, gn_ref[0])
    else:
        x2_ref[0] = x2
        sh1, sc1 = _mod_parts(modn_ref[0])[:2]
        hn_ref[0] = _norm_mod(x2, gn_ref[0], sc1, sh1).astype(BF16)


def _mlp_call(y_ssd, y_mla, y_swa, x, mod3, wp, next_g, l, next_l, tm, last):
    bsz, s, d = x.shape
    tok = lambda width: pl.BlockSpec((1, tm, width), lambda b, i: (b, i, 0))
    once = pl.Buffered(1)
    kern = functools.partial(_mlp_kernel, tm=tm, last=last)
    gl = 0 if last else next_l
    outs = pl.pallas_call(
        kern,
        grid=(bsz, s // tm),
        in_specs=[
            tok(SSD_INNER),
            tok(MLA_HEADS * MLA_V),
            tok(SWA_HEADS * SWA_HEAD_DIM),
            tok(d),
            pl.BlockSpec((1, 1, 6 * d), lambda b, i: (l * bsz + b, 0, 0)),
            pl.BlockSpec((1, 1, 6 * d), lambda b, i: (next_l * bsz + b, 0, 0)),
            pl.BlockSpec((1, d, d), lambda b, i: (l, 0, 0), pipeline_mode=once),
            pl.BlockSpec((1, 1, d), lambda b, i: (l, 0, 0)),
            pl.BlockSpec((1, FF_CHUNKS, d, FF_TILE), lambda b, i: (l, 0, 0, 0), pipeline_mode=once),
            pl.BlockSpec((1, FF_CHUNKS, d, FF_TILE), lambda b, i: (l, 0, 0, 0), pipeline_mode=once),
            pl.BlockSpec((1, 2, FF_CHUNKS, FFN_CONV + 1, FF_TILE), lambda b, i: (l, 0, 0, 0, 0)),
            pl.BlockSpec((1, FF_CHUNKS, FF_TILE, d), lambda b, i: (l, 0, 0, 0), pipeline_mode=once),
            pl.BlockSpec((1, 1, d), lambda b, i: (gl, 0, 0)),
        ],
        out_specs=[tok(d)] if last else [tok(d), tok(d)],
        out_shape=([jax.ShapeDtypeStruct((bsz, s, d), F32)] if last else
                   [jax.ShapeDtypeStruct((bsz, s, d), F32), jax.ShapeDtypeStruct((bsz, s, d), BF16)]),
        scratch_shapes=[
            pltpu.VMEM((tm, d), BF16),
            pltpu.VMEM((tm, d), F32),
            pltpu.VMEM((CONV_BUFS, tm + SUBLANES, FF_TILE), F32),
            pltpu.VMEM((CONV_BUFS, tm + SUBLANES, FF_TILE), F32),
            pltpu.VMEM((FF_CHUNKS, SUBLANES, FF_TILE), F32),
            pltpu.VMEM((FF_CHUNKS, SUBLANES, FF_TILE), F32),
            pltpu.VMEM((tm, d), F32),
        ],
        compiler_params=_params(("parallel", "arbitrary")),
        name="mlp",
    )(y_ssd, y_mla, y_swa, x, mod3, mod3, wp["w_out"], wp["norm2_g"], wp["ffn_wa"], wp["ffn_wb"],
      wp["ffn_conv"], wp["ffn_wd"], next_g)
    return (outs[0], None) if last else outs


def _pack_weights(norm1_g, norm2_g, w_in, ssd_conv_w, ssd_conv_b, ssd_dt_bias, ssd_a_log, ssd_d,
                  ssd_norm_g, mla_q_norm_g, mla_w_uq, mla_kv_norm_g, mla_w_ukv, w_out, ffn_w_up,
                  ffn_conv_w, ffn_conv_b, ffn_w_down):
    nl = w_in.shape[0]
    row = lambda v: v.reshape(nl, 1, v.shape[-1])

    o_dt = SSD_INNER + SSD_XBC
    o_cq = SSD_IN
    o_ckv = o_cq + MLA_Q_RANK
    o_kr = o_ckv + MLA_KV_RANK
    o_sq = SSD_IN + MLA_IN
    o_sk = o_sq + SWA_HEADS * SWA_HEAD_DIM
    o_sv = o_sk + SWA_KV_HEADS * SWA_HEAD_DIM
    hd = SWA_HEAD_DIM
    idx = []
    idx += list(range(0, SSD_INNER + SSD_XBC))
    idx += list(range(o_cq, o_cq + MLA_Q_RANK))
    idx += list(range(o_ckv, o_ckv + MLA_KV_RANK))
    idx += [-1] * MLA_NOPE + list(range(o_kr, o_kr + MLA_ROPE)) + [-1] * (LANES - MLA_NOPE - MLA_ROPE)
    for head in (0, 2, 1, 3):
        idx += list(range(o_sq + head * hd, o_sq + (head + 1) * hd))
    idx += list(range(o_sk, o_sk + SWA_KV_HEADS * hd))
    idx += list(range(o_sv, o_sv + SWA_KV_HEADS * hd))
    idx += list(range(o_dt, o_dt + SSD_HEADS)) + [-1] * (LANES - SSD_HEADS)
    assert len(idx) == D_IN_PACKED
    idx = jnp.asarray(idx, jnp.int32)
    w_in_p = jnp.where(idx >= 0, jnp.take(w_in, jnp.maximum(idx, 0), axis=2), 0.0).astype(BF16)

    dq = MLA_NOPE + MLA_ROPE
    w_uq = mla_w_uq.reshape(nl, MLA_Q_RANK, MLA_HEADS, dq)
    w_uq = jnp.pad(w_uq, ((0, 0), (0, 0), (0, 0), (0, LANES - dq))).reshape(nl, MLA_Q_RANK, MLA_HEADS * LANES)
    w_ukv = mla_w_ukv.reshape(nl, MLA_KV_RANK, MLA_HEADS, MLA_NOPE + MLA_V)
    w_uk = jnp.pad(w_ukv[..., :MLA_NOPE], ((0, 0), (0, 0), (0, 0), (0, LANES - MLA_NOPE)))
    w_uk = w_uk.reshape(nl, MLA_KV_RANK, MLA_HEADS * LANES)
    w_uv = w_ukv[..., MLA_NOPE:].reshape(nl, MLA_KV_RANK, MLA_HEADS * MLA_V)

    o_w = SSD_INNER + MLA_HEADS * MLA_V
    w_swa = w_out[:, o_w:, :].reshape(nl, SWA_HEADS, hd, D_MODEL)[:, jnp.asarray([0, 2, 1, 3])]
    w_out_p = jnp.concatenate([w_out[:, :o_w, :], w_swa.reshape(nl, SWA_HEADS * hd, D_MODEL)], axis=1)

    wa = ffn_w_up[:, :, :D_FF].reshape(nl, D_MODEL, FF_CHUNKS, FF_TILE).transpose(0, 2, 1, 3)
    wb = ffn_w_up[:, :, D_FF:].reshape(nl, D_MODEL, FF_CHUNKS, FF_TILE).transpose(0, 2, 1, 3)
    taps = jnp.concatenate([ffn_conv_w, ffn_conv_b[:, None, :]], axis=1)
    taps = taps.reshape(nl, FFN_CONV + 1, 2, FF_CHUNKS, FF_TILE).transpose(0, 2, 3, 1, 4)
    wd = ffn_w_down.reshape(nl, FF_CHUNKS, FF_TILE, D_MODEL)

    return {
        "norm1_g": row(norm1_g), "norm2_g": row(norm2_g),
        "w_in": w_in_p,
        "ssd_conv_w": ssd_conv_w, "ssd_conv_b": row(ssd_conv_b),
        "dt_bias": ssd_dt_bias.reshape(nl, SSD_HEADS, 1),
        "ssd_a_log": ssd_a_log.reshape(nl, SSD_HEADS, 1),
        "ssd_d_lanes": row(jnp.repeat(ssd_d, SSD_HEAD_DIM, axis=-1)),
        "ssd_norm_g": row(ssd_norm_g),
        "q_norm_g": row(mla_q_norm_g), "w_uq": w_uq.astype(BF16),
        "kv_norm_g": row(mla_kv_norm_g), "w_uk": w_uk.astype(BF16), "w_uv": w_uv.astype(BF16),
        "w_out": w_out_p.astype(BF16),
        "ffn_wa": wa.astype(BF16), "ffn_wb": wb.astype(BF16), "ffn_conv": taps, "ffn_wd": wd.astype(BF16),
    }


def _tile(s, want):
    t = min(s, want)
    assert s % t == 0
    return t


def kernel(x, c, positions, ada_w, ada_b, norm1_g, norm2_g, w_in, ssd_conv_w, ssd_conv_b, ssd_dt_bias,
           ssd_a_log, ssd_d, ssd_norm_g, mla_q_norm_g, mla_w_uq, mla_kv_norm_g, mla_w_ukv, swa_sinks,
           w_out, ffn_w_up, ffn_conv_w, ffn_conv_b, ffn_w_down, final_norm_g):
    bsz, s, d = x.shape
    nl = w_in.shape[0]
    assert d == D_MODEL and s % SSD_CHUNK == 0
    tm = _tile(s, 512)
    tf = _tile(s, 512)
    ts = _tile(s, 512)
    tq = _tile(s, 512)

    wp = _pack_weights(norm1_g, norm2_g, w_in, ssd_conv_w, ssd_conv_b, ssd_dt_bias, ssd_a_log, ssd_d,
                       ssd_norm_g, mla_q_norm_g, mla_w_uq, mla_kv_norm_g, mla_w_ukv, w_out, ffn_w_up,
                       ffn_conv_w, ffn_conv_b, ffn_w_down)
    final_g = final_norm_g.reshape(1, 1, d)
    mod3 = _ada_call(c, ada_w, ada_b).reshape(nl * bsz, 1, 6 * d)
    ct, st = _rope_call(positions)

    h = _prenorm_call(x, mod3, wp["norm1_g"], 0, tm)
    for l in range(nl):
        z, xbc, dt, qm, km, vm, sq, sk, sv = _inproj_call(h, wp, ct, st, l, tm)
        y_ssd = _ssd_call(z, xbc, dt, wp, l, ts)
        y_mla = _mla_call(qm, km, vm, tq)
        y_swa = _swa_call(sq, sk, sv, swa_sinks, l, ts)
        last = l == nl - 1
        next_l = l if last else l + 1
        next_g = final_g if last else wp["norm1_g"]
        x, h = _mlp_call(y_ssd, y_mla, y_swa, x, mod3, wp, next_g, l, next_l, tf, last)
    return x
```
